```python
import math
import jax, jax.numpy as jnp
from jax import lax
import numpy as np

D_MODEL = 1024
BATCH = 8
SEQ = 4096
DEPTH = 2

HEAD_DIM = 64
RWKV_HEADS = 6
RWKV_W = RWKV_HEADS * HEAD_DIM
D_DECAY_LORA = 64
D_AAA_LORA = 64
D_GATE_LORA = 128
RWKV_SPLITS = (RWKV_W, RWKV_W, RWKV_W, D_DECAY_LORA, D_AAA_LORA, D_GATE_LORA)
RWKV_COLS = sum(RWKV_SPLITS)
GN_EPS = 64e-5
NSA_Q_HEADS = 6
NSA_KV_GROUPS = 2
NSA_Q_PER_KV = NSA_Q_HEADS // NSA_KV_GROUPS
NSA_W = NSA_Q_HEADS * HEAD_DIM
NSA_KV_W = NSA_KV_GROUPS * HEAD_DIM
NSA_N_BRANCH = 3
NSA_SPLITS = (NSA_W,) + (NSA_KV_W,) * 6 + (NSA_Q_HEADS * NSA_N_BRANCH,)
NSA_COLS = sum(NSA_SPLITS)
CMP_LEN = 32
CMP_STRIDE = 16
CMP_HIDDEN = 128
SEL_LEN = 64
SEL_TOPK = 16
WINDOW = 512
Q_BLOCK = 64
FORCED_BONUS = 1e3
NEG_INF = -1e30
S5_GROUPS = 16
S5_CH = 16
S5_W = S5_GROUPS * S5_CH
S5_STATE = 64
DT_MIN = 0.001
DT_MAX = 0.1
D_MIX = RWKV_W + NSA_W + S5_W
N_IN = RWKV_COLS + NSA_COLS + S5_W
D_FF = 2816
CONV_W = 3
PLE_DIM = 256
NORM_EPS = 1e-6

kernel_name = "hybrid_rwkv7_nsa_s5_sandwich_block"


def _split(z, sizes):
    cuts = [int(c) for c in np.cumsum(sizes)[:-1]]
    return jnp.split(z, cuts, axis=-1)


def rms_norm(x, g):
    xf = x.astype(jnp.float32)
    y = xf * lax.rsqrt(jnp.mean(xf * xf, axis=-1, keepdims=True) + NORM_EPS)
    return (y * g.astype(jnp.float32)).astype(x.dtype)


def masked_softmax(s, mask):
    s = jnp.where(mask, s, NEG_INF)
    return jnp.where(mask, jax.nn.softmax(s, axis=-1), 0.0)


def alibi_slopes(n):
    def pow2(m):
        start = 2.0 ** (-8.0 / m)
        return [start ** (i + 1) for i in range(m)]
    if math.log2(n).is_integer():
        return pow2(n)
    c = 2 ** math.floor(math.log2(n))
    return pow2(c) + pow2(2 * c)[0::2][: n - c]


def token_shift(z, mu):
    prev = jnp.pad(z, ((0, 0), (1, 0), (0, 0)))[:, :-1]
    return z + (prev - z) * mu


def rwkv7_mixer(z, w0, w2, a0, a2, g2, k_k, k_a, r_k, gn_w, gn_b):
    B, T, _ = z.shape
    H, N = RWKV_HEADS, HEAD_DIM
    z = z.astype(jnp.float32)
    r, k, v, wl, al, gl = _split(z, RWKV_SPLITS)
    w = -jax.nn.softplus(-(w0 + jnp.tanh(wl) @ w2)) - 0.5
    a = jax.nn.sigmoid(a0 + al @ a2)
    g = jax.nn.sigmoid(gl) @ g2
    heads = lambda t: t.reshape(B, T, H, N)
    kk = heads(k * k_k)
    kk = kk / jnp.maximum(jnp.sqrt(jnp.sum(kk * kk, axis=-1, keepdims=True)), 1e-12)
    k = k * (1.0 + (a - 1.0) * k_a)
    decay = jnp.exp(-jnp.exp(w))
    r_h, k_h, v_h, a_h, d_h = (heads(t) for t in (r, k, v, a, decay))

    def step(S, inp):
        r_t, d_t, k_t, v_t, kk_t, a_t = inp
        sa = jnp.einsum('bhij,bhj->bhi', S, -kk_t)
        S = S * d_t[:, :, None, :] + sa[..., None] * (kk_t * a_t)[:, :, None, :] + v_t[..., None] * k_t[:, :, None, :]
        return S, jnp.einsum('bhij,bhj->bhi', S, r_t)

    tm = lambda t: jnp.moveaxis(t, 1, 0)
    S0 = jnp.zeros((B, H, N, N), jnp.float32)
    _, o = lax.scan(step, S0, tuple(tm(t) for t in (r_h, d_h, k_h, v_h, kk, a_h)))
    o = jnp.moveaxis(o, 0, 1)
    mu = jnp.mean(o, axis=-1, keepdims=True)
    var = jnp.mean(jnp.square(o - mu), axis=-1, keepdims=True)
    o = ((o - mu) * lax.rsqrt(var + GN_EPS)).reshape(B, T, RWKV_W) * gn_w + gn_b
    bonus = jnp.sum(r_h * k_h * r_k.reshape(H, N), axis=-1, keepdims=True) * v_h
    return (o + bonus.reshape(B, T, RWKV_W)) * g


def nsa_mixer(z, pe_k, pe_v, w1_k, w2_k, w1_v, w2_v):
    B, T, _ = z.shape
    G, R, Dh = NSA_KV_GROUPS, NSA_Q_PER_KV, HEAD_DIM
    f32 = jnp.float32
    z = z.astype(f32)
    q, kc, vc, ks, vs, kw, vw, gl = _split(z, NSA_SPLITS)
    kv_heads = lambda t: t.reshape(B, T, G, Dh).transpose(0, 2, 1, 3)
    kc, vc, ks, vs, kw, vw = (kv_heads(t) for t in (kc, vc, ks, vs, kw, vw))

    n_cmp = (T - CMP_LEN) // CMP_STRIDE + 1
    cmp_start = jnp.arange(n_cmp) * CMP_STRIDE
    cmp_end = cmp_start + CMP_LEN - 1
    cmp_idx = cmp_start[:, None] + jnp.arange(CMP_LEN)[None, :]

    def compress(t, pe, w1, w2):
        blocks = (t[:, :, cmp_idx] + pe.astype(f32)).reshape(B, G, n_cmp, CMP_LEN * Dh)
        return jax.nn.gelu(blocks @ w1.astype(f32)) @ w2.astype(f32)

    k_cmp = compress(kc, pe_k, w1_k, w2_k)
    v_cmp = compress(vc, pe_v, w1_v, w2_v)

    n_sel = T // SEL_LEN
    k_top = min(SEL_TOPK, n_sel)
    ks_blk = ks.reshape(B, G, n_sel, SEL_LEN, Dh)
    vs_blk = vs.reshape(B, G, n_sel, SEL_LEN, Dh)
    blk_ids = jnp.arange(n_sel)
    blk_start = blk_ids * SEL_LEN
    overlap = ((cmp_start[:, None] < blk_start[None, :] + SEL_LEN)
               & (cmp_end[:, None] >= blk_start[None, :])).astype(f32)
    gather_blocks = jax.vmap(jax.vmap(lambda blk, sel: blk[sel]))

    win_len = WINDOW + Q_BLOCK - 1
    kw_pad = jnp.pad(kw, ((0, 0), (0, 0), (WINDOW, 0), (0, 0)))
    vw_pad = jnp.pad(vw, ((0, 0), (0, 0), (WINDOW, 0), (0, 0)))

    slopes = jnp.asarray(alibi_slopes(NSA_Q_HEADS), f32).reshape(G, R)[None, :, :, None, None]
    scale = HEAD_DIM ** -0.5
    n_qb = T // Q_BLOCK
    q_blocks = q.reshape(B, n_qb, Q_BLOCK, G, R, Dh).transpose(1, 0, 3, 4, 2, 5)
    g_blocks = jax.nn.sigmoid(gl).reshape(B, n_qb, Q_BLOCK, G, R, NSA_N_BRANCH).transpose(1, 0, 3, 4, 2, 5)

    def query_block(args):
        qb, gb, bi = args
        t0 = bi * Q_BLOCK
        t = t0 + jnp.arange(Q_BLOCK)
        d_c = t[:, None] - cmp_end[None, :]
        s_c = jnp.einsum('bgrqd,bgnd->bgrqn', qb, k_cmp) * scale - slopes * jnp.abs(d_c).astype(f32)
        p_c = masked_softmax(s_c, d_c >= 0)
        o_c = jnp.einsum('bgrqn,bgnd->bgrqd', p_c, v_cmp)
        imp = jnp.einsum('bgrqn,nj->bgqj', p_c, overlap)
        cur = t // SEL_LEN
        valid = blk_start[None, :] <= t[:, None]
        forced = (blk_ids[None, :] == 0) | (blk_ids[None, :] == cur[:, None]) | (blk_ids[None, :] == cur[:, None] - 1)
        score = jnp.where(valid, imp + FORCED_BONUS * forced.astype(f32), -jnp.inf)
        _, sel = lax.top_k(score, k_top)
        k_sel = gather_blocks(ks_blk, sel)
        v_sel = gather_blocks(vs_blk, sel)
        pos = sel[..., None] * SEL_LEN + jnp.arange(SEL_LEN)
        d_s = (t[:, None, None] - pos)[:, :, None]
        s_s = jnp.einsum('bgrqd,bgqnld->bgrqnl', qb, k_sel) * scale - slopes[..., None] * jnp.abs(d_s).astype(f32)
        p_s = masked_softmax(s_s.reshape(B, G, R, Q_BLOCK, k_top * SEL_LEN),
                             (d_s >= 0).reshape(B, G, 1, Q_BLOCK, k_top * SEL_LEN))
        o_s = jnp.einsum('bgrqnl,bgqnld->bgrqd', p_s.reshape(s_s.shape), v_sel)
        k_win = lax.dynamic_slice_in_dim(kw_pad, t0 + 1, win_len, axis=2)
        v_win = lax.dynamic_slice_in_dim(vw_pad, t0 + 1, win_len, axis=2)
        pos_w = t0 - WINDOW + 1 + jnp.arange(win_len)
        d_w = t[:, None] - pos_w[None, :]
        mask_w = (d_w >= 0) & (d_w < WINDOW) & (pos_w[None, :] >= 0)
        s_w = jnp.einsum('bgrqd,bgkd->bgrqk', qb, k_win) * scale - slopes * jnp.abs(d_w).astype(f32)
        o_w = jnp.einsum('bgrqk,bgkd->bgrqd', masked_softmax(s_w, mask_w), v_win)
        return gb[..., 0:1] * o_c + gb[..., 1:2] * o_s + gb[..., 2:3] * o_w

    o = lax.map(query_block, (q_blocks, g_blocks, jnp.arange(n_qb)))
    return o.transpose(1, 0, 4, 2, 3, 5).reshape(B, T, NSA_W)


def _complex_linear_combine(e1, e2):
    a1r, a1i, b1r, b1i = e1
    a2r, a2i, b2r, b2i = e2
    return (a1r * a2r - a1i * a2i, a1r * a2i + a1i * a2r,
            a2r * b1r - a2i * b1i + b2r, a2r * b1i + a2i * b1r + b2i)


def s5_mixer(u, lam_re, lam_im, log_dt, b_re, b_im, c_re, c_im, d_skip, w_glu):
    B, T, _ = u.shape
    f32 = jnp.float32
    u = u.astype(f32).reshape(B, T, S5_GROUPS, S5_CH)
    lam_re, lam_im = lam_re.astype(f32), lam_im.astype(f32)
    b_re, b_im, c_re, c_im = (t.astype(f32) for t in (b_re, b_im, c_re, c_im))
    dt = jnp.exp(log_dt.astype(f32))[:, None]
    mag = jnp.exp(lam_re * dt)
    ab_re, ab_im = mag * jnp.cos(lam_im * dt), mag * jnp.sin(lam_im * dt)
    den = lam_re * lam_re + lam_im * lam_im
    f_re = ((ab_re - 1.0) * lam_re + ab_im * lam_im) / den
    f_im = (ab_im * lam_re - (ab_re - 1.0) * lam_im) / den
    bb_re = f_re[..., None] * b_re - f_im[..., None] * b_im
    bb_im = f_re[..., None] * b_im + f_im[..., None] * b_re
    bu_re = jnp.einsum('gpc,btgc->btgp', bb_re, u)
    bu_im = jnp.einsum('gpc,btgc->btgp', bb_im, u)
    a_re = jnp.broadcast_to(ab_re, bu_re.shape)
    a_im = jnp.broadcast_to(ab_im, bu_im.shape)
    _, _, s_re, s_im = lax.associative_scan(_complex_linear_combine, (a_re, a_im, bu_re, bu_im), axis=1)
    y = (jnp.einsum('gcp,btgp->btgc', c_re, s_re) - jnp.einsum('gcp,btgp->btgc', c_im, s_im)
         + d_skip.astype(f32).reshape(S5_GROUPS, S5_CH) * u)
    y = jax.nn.gelu(y.reshape(B, T, S5_W))
    val, gate = jnp.split(y @ w_glu, 2, axis=-1)
    return val * jax.nn.sigmoid(gate)


def conv_ffn(x, w_up, conv_w, conv_b, w_down):
    hu = x @ w_up
    c = hu.shape[-1]
    hu = lax.conv_general_dilated(hu, conv_w[:, None, :], window_strides=(1,), padding=[(CONV_W - 1, 0)],
                                  dimension_numbers=('NWC', 'WIO', 'NWC'), feature_group_count=c) + conv_b
    gate, up = jnp.split(hu, 2, axis=-1)
    return (jax.nn.gelu(gate, approximate=True) * up) @ w_down


def setup_inputs(seed: int = 0) -> dict:
    key = jax.random.key(seed)
    keys = iter(jax.random.split(key, 48))
    f32 = jnp.float32

    def nrm(shape, scale):
        return scale * jax.random.normal(next(keys), shape, f32)

    def gain(shape):
        return 1.0 + 0.02 * jax.random.normal(next(keys), shape, f32)

    def unif(shape, lo, hi):
        return jax.random.uniform(next(keys), shape, f32, lo, hi)

    L = DEPTH
    n = jnp.arange(S5_STATE, dtype=f32)
    return {
        'x': nrm((BATCH, SEQ, D_MODEL), 1.0),
        'p': nrm((DEPTH, BATCH, SEQ, PLE_DIM), 1.0),
        'pre_mix_norm': gain((L, D_MODEL)),
        'post_mix_norm': gain((L, D_MODEL)),
        'pre_ffn_norm': gain((L, D_MODEL)),
        'post_ffn_norm': gain((L, D_MODEL)),
        'w_in': nrm((L, D_MODEL, N_IN), D_MODEL ** -0.5),
        'w_out': nrm((L, D_MIX, D_MODEL), D_MIX ** -0.5),
        'shift_mu': unif((L, RWKV_COLS), 0.0, 1.0),
        'rw_w0': unif((L, RWKV_W), -5.0, -1.0),
        'rw_w2': nrm((L, D_DECAY_LORA, RWKV_W), 0.1),
        'rw_a0': nrm((L, RWKV_W), 0.1),
        'rw_a2': nrm((L, D_AAA_LORA, RWKV_W), 0.3 * D_AAA_LORA ** -0.5),
        'rw_g2': nrm((L, D_GATE_LORA, RWKV_W), D_GATE_LORA ** -0.5),
        'rw_k_k': 0.85 + nrm((L, RWKV_W), 0.02),
        'rw_k_a': gain((L, RWKV_W)),
        'rw_r_k': nrm((L, RWKV_W), 0.1),
        'rw_gn_w': gain((L, RWKV_W)),
        'rw_gn_b': nrm((L, RWKV_W), 0.02),
        'cmp_pe_k': nrm((L, CMP_LEN, HEAD_DIM), 0.02),
        'cmp_pe_v': nrm((L, CMP_LEN, HEAD_DIM), 0.02),
        'cmp_w1_k': nrm((L, CMP_LEN * HEAD_DIM, CMP_HIDDEN), (CMP_LEN * HEAD_DIM) ** -0.5),
        'cmp_w2_k': nrm((L, CMP_HIDDEN, HEAD_DIM), CMP_HIDDEN ** -0.5),
        'cmp_w1_v': nrm((L, CMP_LEN * HEAD_DIM, CMP_HIDDEN), (CMP_LEN * HEAD_DIM) ** -0.5),
        'cmp_w2_v': nrm((L, CMP_HIDDEN, HEAD_DIM), CMP_HIDDEN ** -0.5),
        's5_lam_re': -0.5 + nrm((L, S5_GROUPS, S5_STATE), 0.01),
        's5_lam_im': math.pi * n + nrm((L, S5_GROUPS, S5_STATE), 0.01),
        's5_log_dt': unif((L, S5_GROUPS), math.log(DT_MIN), math.log(DT_MAX)),
        's5_b_re': nrm((L, S5_GROUPS, S5_STATE, S5_CH), (2 * S5_CH) ** -0.5),
        's5_b_im': nrm((L, S5_GROUPS, S5_STATE, S5_CH), (2 * S5_CH) ** -0.5),
        's5_c_re': nrm((L, S5_GROUPS, S5_CH, S5_STATE), (2 * S5_STATE) ** -0.5),
        's5_c_im': nrm((L, S5_GROUPS, S5_CH, S5_STATE), (2 * S5_STATE) ** -0.5),
        's5_d': nrm((L, S5_W), 1.0),
        's5_w_glu': nrm((L, S5_W, 2 * S5_W), S5_W ** -0.5),
        'w_up': nrm((L, D_MODEL, 2 * D_FF), D_MODEL ** -0.5),
        'conv_w': nrm((L, CONV_W, 2 * D_FF), CONV_W ** -0.5),
        'conv_b': nrm((L, 2 * D_FF), 0.02),
        'w_down': nrm((L, D_FF, D_MODEL), D_FF ** -0.5),
        'w_ple': nrm((L, PLE_DIM, D_MODEL), PLE_DIM ** -0.5),
        'ple_norm': gain((L, D_MODEL)),
        'w_ple_gate': nrm((L, D_MODEL, D_MODEL), D_MODEL ** -0.5),
    }


def reference(x, p, pre_mix_norm, post_mix_norm, pre_ffn_norm, post_ffn_norm, w_in, w_out,
              shift_mu, rw_w0, rw_w2, rw_a0, rw_a2, rw_g2, rw_k_k, rw_k_a, rw_r_k, rw_gn_w, rw_gn_b,
              cmp_pe_k, cmp_pe_v, cmp_w1_k, cmp_w2_k, cmp_w1_v, cmp_w2_v,
              s5_lam_re, s5_lam_im, s5_log_dt, s5_b_re, s5_b_im, s5_c_re, s5_c_im, s5_d, s5_w_glu,
              w_up, conv_w, conv_b, w_down, w_ple, ple_norm, w_ple_gate):
    h = x
    for i in range(DEPTH):
        z = rms_norm(h, pre_mix_norm[i]) @ w_in[i]
        z_rw, z_nsa, z_s5 = _split(z, (RWKV_COLS, NSA_COLS, S5_W))
        o_rw = rwkv7_mixer(token_shift(z_rw, shift_mu[i]), rw_w0[i], rw_w2[i], rw_a0[i], rw_a2[i], rw_g2[i],
                           rw_k_k[i], rw_k_a[i], rw_r_k[i], rw_gn_w[i], rw_gn_b[i])
        o_nsa = nsa_mixer(z_nsa, cmp_pe_k[i], cmp_pe_v[i], cmp_w1_k[i], cmp_w2_k[i], cmp_w1_v[i], cmp_w2_v[i])
        o_s5 = s5_mixer(z_s5, s5_lam_re[i], s5_lam_im[i], s5_log_dt[i], s5_b_re[i], s5_b_im[i],
                        s5_c_re[i], s5_c_im[i], s5_d[i], s5_w_glu[i])
        mix = jnp.concatenate([o_rw, o_nsa, o_s5], axis=-1).astype(h.dtype) @ w_out[i]
        h = h + rms_norm(mix, post_mix_norm[i])
        f = conv_ffn(rms_norm(h, pre_ffn_norm[i]), w_up[i], conv_w[i], conv_b[i], w_down[i])
        h = h + rms_norm(f, post_ffn_norm[i])
        e = rms_norm(p[i] @ w_ple[i], ple_norm[i])
        gate = jax.nn.sigmoid((h @ w_ple_gate[i]).astype(jnp.float32)).astype(h.dtype)
        h = h + gate * e
    return h
```

```python
import functools
import math

import numpy as np
import jax
import jax.numpy as jnp
from jax import lax
from jax.experimental import pallas as pl
from jax.experimental.pallas import tpu as pltpu

f32 = jnp.float32
bf16 = jnp.bfloat16
HI = lax.Precision.HIGHEST

D_MODEL = 1024
HEAD_DIM = 64
RWKV_HEADS = 6
RWKV_W = RWKV_HEADS * HEAD_DIM
D_DECAY_LORA = 64
D_AAA_LORA = 64
D_GATE_LORA = 128
RWKV_COLS = 3 * RWKV_W + D_DECAY_LORA + D_AAA_LORA + D_GATE_LORA
GN_EPS = 64e-5
NSA_Q_HEADS = 6
NSA_KV_GROUPS = 2
NSA_Q_PER_KV = NSA_Q_HEADS // NSA_KV_GROUPS
NSA_W = NSA_Q_HEADS * HEAD_DIM
NSA_KV_W = NSA_KV_GROUPS * HEAD_DIM
NSA_N_BRANCH = 3
NSA_COLS = NSA_W + 6 * NSA_KV_W + NSA_Q_HEADS * NSA_N_BRANCH
CMP_LEN = 32
CMP_STRIDE = 16
CMP_HIDDEN = 128
SEL_LEN = 64
SEL_TOPK = 16
WINDOW = 512
FORCED_BONUS = 1e3
NEG_INF = -1e30
S5_GROUPS = 16
S5_CH = 16
S5_W = S5_GROUPS * S5_CH
S5_STATE = 64
D_MIX = RWKV_W + NSA_W + S5_W
D_FF = 2816
CONV_W = 3
PLE_DIM = 256
NORM_EPS = 1e-6

LANES = 128
SUBLANES = 8
VMEM_LIMIT = 56 * 1024 * 1024

RW_LORA_W_OFF = 3 * RWKV_W
RW_LORA_A_OFF = RW_LORA_W_OFF + LANES
RW_LORA_G_OFF = RW_LORA_A_OFF + LANES
RW_P = RW_LORA_G_OFF + D_GATE_LORA
NSA_GATE_OFF = NSA_W + 6 * NSA_KV_W
NSA_P = NSA_GATE_OFF + LANES
IN_P = RW_P + NSA_P + S5_W

IN_TM = 256
RW_CHUNK = 64
NSA_QT = 128
NSA_KT = 256
S5_CHUNK = 64
OUT_TM = 512
FFN_TM = 512
FFN_TC = 256
PLE_TM = 512
S5P_TM = 1024


def _cparams(*sem):
    return pltpu.CompilerParams(dimension_semantics=sem, vmem_limit_bytes=VMEM_LIMIT)


def _rms(x, g):
    ms = jnp.mean(x * x, axis=-1, keepdims=True)
    return x * lax.rsqrt(ms + NORM_EPS) * g


def _gelu_tanh(x):
    return 0.5 * x * (1.0 + jnp.tanh(math.sqrt(2.0 / math.pi) * (x + 0.044715 * (x * x * x))))


def _sigmoid(x):
    return 1.0 / (1.0 + jnp.exp(-x))


def _softplus(x):
    return jnp.maximum(x, 0.0) + jnp.log(1.0 + jnp.exp(-jnp.abs(x)))


def _dot(a, b):
    return jnp.dot(a.astype(bf16), b.astype(bf16), preferred_element_type=f32)


def _dot_nt(a, b):
    return lax.dot_general(a.astype(bf16), b.astype(bf16), (((1,), (1,)), ((), ())), preferred_element_type=f32)


def _dot_tn(a, b):
    return lax.dot_general(a.astype(bf16), b.astype(bf16), (((0,), (0,)), ((), ())), preferred_element_type=f32)


def _dot_hi(a, b):
    return jnp.dot(a, b, preferred_element_type=f32, precision=HI)


def _in_proj_body(x_ref, g_ref, w_ref, zrw_ref, znsa_ref, zs5_ref):
    xn = _rms(x_ref[...], g_ref[...]).astype(bf16)
    zrw_ref[...] = jnp.dot(xn, w_ref[:, 0:RW_P], preferred_element_type=f32)
    znsa_ref[...] = jnp.dot(xn, w_ref[:, RW_P:RW_P + NSA_P], preferred_element_type=f32)
    zs5_ref[...] = jnp.dot(xn, w_ref[:, RW_P + NSA_P:IN_P], preferred_element_type=f32)


def _in_proj(h2d, g, w_p):
    n = h2d.shape[0]
    tm = min(IN_TM, n)
    return pl.pallas_call(
        _in_proj_body,
        grid=(n // tm,),
        in_specs=[pl.BlockSpec((tm, D_MODEL), lambda i: (i, 0)),
                  pl.BlockSpec((1, D_MODEL), lambda i: (0, 0)),
                  pl.BlockSpec((D_MODEL, IN_P), lambda i: (0, 0))],
        out_specs=[pl.BlockSpec((tm, RW_P), lambda i: (i, 0)),
                   pl.BlockSpec((tm, NSA_P), lambda i: (i, 0)),
                   pl.BlockSpec((tm, S5_W), lambda i: (i, 0))],
        out_shape=[jax.ShapeDtypeStruct((n, RW_P), f32),
                   jax.ShapeDtypeStruct((n, NSA_P), f32),
                   jax.ShapeDtypeStruct((n, S5_W), f32)],
        compiler_params=_cparams("parallel"),
        name="in_proj",
    )(h2d, g.reshape(1, D_MODEL), w_p)


def _pad_in_weights(w_in, shift_mu):
    d = w_in.shape[0]
    zc = lambda k: jnp.zeros((d, k), w_in.dtype)
    o = 3 * RWKV_W
    rw = [w_in[:, :o],
          w_in[:, o:o + D_DECAY_LORA], zc(LANES - D_DECAY_LORA),
          w_in[:, o + D_DECAY_LORA:o + D_DECAY_LORA + D_AAA_LORA], zc(LANES - D_AAA_LORA),
          w_in[:, o + D_DECAY_LORA + D_AAA_LORA:RWKV_COLS]]
    nsa = [w_in[:, RWKV_COLS:RWKV_COLS + NSA_COLS], zc(NSA_P - NSA_COLS)]
    s5 = [w_in[:, RWKV_COLS + NSA_COLS:]]
    w_p = jnp.concatenate(rw + nsa + s5, axis=1).astype(bf16)
    z1 = lambda k: jnp.zeros((k,), shift_mu.dtype)
    mu_p = jnp.concatenate([shift_mu[:o],
                            shift_mu[o:o + D_DECAY_LORA], z1(LANES - D_DECAY_LORA),
                            shift_mu[o + D_DECAY_LORA:o + D_DECAY_LORA + D_AAA_LORA], z1(LANES - D_AAA_LORA),
                            shift_mu[o + D_DECAY_LORA + D_AAA_LORA:]])
    return w_p, mu_p.reshape(1, RW_P)


def _rwkv_body(z_ref, mu_ref, w0_ref, w2_ref, a0_ref, a2_ref, g2_ref, kk_ref, ka_ref, rk_ref, gnw_ref, gnb_ref,
               bd_ref, o_ref, state_ref, prev_ref):
    C = RW_CHUNK
    N = HEAD_DIM

    @pl.when(pl.program_id(1) == 0)
    def _():
        state_ref[...] = jnp.zeros_like(state_ref)
        prev_ref[...] = jnp.zeros_like(prev_ref)

    z = z_ref[...]
    row = lax.broadcasted_iota(jnp.int32, z.shape, 0)
    zprev = jnp.where(row == 0, prev_ref[0:1, :], pltpu.roll(z, 1, 0))
    prev_ref[0:1, :] = z[C - 1:C, :]
    zs = z + (zprev - z) * mu_ref[...]

    r = zs[:, 0:RWKV_W]
    k = zs[:, RWKV_W:2 * RWKV_W]
    v = zs[:, 2 * RWKV_W:3 * RWKV_W]
    wl = zs[:, RW_LORA_W_OFF:RW_LORA_W_OFF + LANES]
    al = zs[:, RW_LORA_A_OFF:RW_LORA_A_OFF + LANES]
    gl = zs[:, RW_LORA_G_OFF:RW_LORA_G_OFF + D_GATE_LORA]
    bd = bd_ref[...]

    w = -_softplus(-(w0_ref[...] + _dot(jnp.tanh(wl), w2_ref[...]))) - 0.5
    logd = -jnp.exp(w)
    a = _sigmoid(a0_ref[...] + _dot(al, a2_ref[...]))
    g = _dot(_sigmoid(gl), g2_ref[...])
    kkr = k * kk_ref[...]
    kk = kkr / jnp.maximum(jnp.sqrt(_dot_hi(kkr * kkr, bd)), 1e-12)
    kmod = k * (1.0 + (a - 1.0) * ka_ref[...])
    b = kk * a

    ti = lax.broadcasted_iota(jnp.int32, (C, C), 0)
    si = lax.broadcasted_iota(jnp.int32, (C, C), 1)
    tri_incl = (si <= ti).astype(f32)
    tri_strict = (si < ti).astype(f32)
    eye = (si == ti).astype(f32)
    cum = _dot_hi(tri_incl, logd)
    tot = cum[C - 1:C, :]
    e_in = jnp.exp(cum)
    e_ex = jnp.exp(cum - logd)
    e_inv = jnp.exp(-cum)
    e_end = jnp.exp(tot - cum)
    r_t = r * e_in
    kk_t = kk * e_ex
    k_h = kmod * e_inv
    b_h = b * e_inv
    k_e = kmod * e_end
    b_e = b * e_end
    e_tot = jnp.exp(tot)

    lvl_masks = []
    s = 1
    while s < C:
        sh = s.bit_length() - 1
        m = ((ti >> (sh + 1)) == (si >> (sh + 1))) & (((ti >> sh) & 1) == 1) & (((si >> sh) & 1) == 0)
        lvl_masks.append(m.astype(f32))
        s *= 2

    outs = []
    for h in range(RWKV_HEADS):
        sl = slice(h * N, (h + 1) * N)
        p = _dot_nt(jnp.concatenate([kk_t[:, sl], r_t[:, sl]], axis=0),
                    jnp.concatenate([b_h[:, sl], k_h[:, sl]], axis=0))
        a_ab = p[0:C, 0:C] * tri_strict
        a_ak = p[0:C, C:2 * C] * tri_strict
        a_rb = p[C:2 * C, 0:C] * tri_incl
        a_rk = p[C:2 * C, C:2 * C] * tri_incl
        x = eye - a_ab * lvl_masks[0]
        for m in lvl_masks[1:]:
            x = x - _dot(_dot(x, a_ab * m), x)
        vh = v[:, sl]
        av = _dot(jnp.concatenate([a_ak, a_rk], axis=0), vh)
        wu = _dot(x, jnp.concatenate([kk_t[:, sl], av[0:C]], axis=1))
        bwu = _dot_tn(b_e[:, sl], wu)
        ktv = _dot_tn(k_e[:, sl], vh)
        arb = _dot(a_rb, wu)
        eyen = eye[0:N, 0:N]
        tm = eyen * e_tot[:, sl] - bwu[:, 0:N]
        zc = ktv - bwu[:, N:2 * N]
        q = r_t[:, sl] - arb[:, 0:N]
        o_in = av[C:2 * C] - arb[:, N:2 * N]
        s0 = state_ref[h]
        ts = _dot(jnp.concatenate([tm, q], axis=0), s0)
        state_ref[h] = ts[0:N] + zc
        outs.append(ts[N:N + C] + o_in)
    o = jnp.concatenate(outs, axis=1)

    inv_n = 1.0 / N
    mean = _dot_hi(o, bd) * inv_n
    oc = o - mean
    var = _dot_hi(oc * oc, bd) * inv_n
    on = oc * lax.rsqrt(var + GN_EPS) * gnw_ref[...] + gnb_ref[...]
    bonus = _dot_hi(r * kmod * rk_ref[...], bd) * v
    o_ref[...] = (on + bonus) * g


def _rwkv(z_rw, mu_p, w0, w2, a0, a2, g2, k_k, k_a, r_k, gn_w, gn_b):
    B, T, _ = z_rw.shape
    C = RW_CHUNK
    row = lambda t: t.reshape(1, RWKV_W)
    w2p = jnp.concatenate([w2, jnp.zeros((LANES - D_DECAY_LORA, RWKV_W), w2.dtype)], axis=0).astype(bf16)
    a2p = jnp.concatenate([a2, jnp.zeros((LANES - D_AAA_LORA, RWKV_W), a2.dtype)], axis=0).astype(bf16)
    hid = np.arange(RWKV_W) // HEAD_DIM
    bd = jnp.asarray((hid[:, None] == hid[None, :]).astype(np.float32))
    const = lambda shp: pl.BlockSpec(shp, lambda b, c: (0,) * len(shp))
    return pl.pallas_call(
        _rwkv_body,
        grid=(B, T // C),
        in_specs=[pl.BlockSpec((None, C, RW_P), lambda b, c: (b, c, 0)),
                  const((1, RW_P)), const((1, RWKV_W)), const((LANES, RWKV_W)), const((1, RWKV_W)),
                  const((LANES, RWKV_W)), const((D_GATE_LORA, RWKV_W)), const((1, RWKV_W)), const((1, RWKV_W)),
                  const((1, RWKV_W)), const((1, RWKV_W)), const((1, RWKV_W)), const((RWKV_W, RWKV_W))],
        out_specs=pl.BlockSpec((None, C, RWKV_W), lambda b, c: (b, c, 0)),
        out_shape=jax.ShapeDtypeStruct((B, T, RWKV_W), f32),
        scratch_shapes=[pltpu.VMEM((RWKV_HEADS, HEAD_DIM, HEAD_DIM), f32),
                        pltpu.VMEM((SUBLANES, RW_P), f32)],
        compiler_params=_cparams("parallel", "arbitrary"),
        name="rwkv7",
    )(z_rw, mu_p, row(w0), w2p, row(a0), a2p, g2.astype(bf16), row(k_k), row(k_a), row(r_k), row(gn_w), row(gn_b), bd)


def _cmp_body(k2_ref, v2_ref, pek_ref, pev_ref, w1k_ref, w2k_ref, w1v_ref, w2v_ref, ko_ref, vo_ref):
    half = (CMP_LEN // 2) * HEAD_DIM

    def one(x2_ref, pe_ref, w1_ref, w2_ref, out_ref):
        x2 = x2_ref[...]
        nrow = x2.shape[0]
        ha = jnp.dot(x2, w1_ref[0:half, :], preferred_element_type=f32)
        hb = jnp.dot(x2, w1_ref[half:2 * half, :], preferred_element_type=f32)
        hb_next = pltpu.roll(hb, nrow - 1, 0)
        c0 = jnp.dot(pe_ref[...], w1_ref[...], preferred_element_type=f32)
        hid = _gelu_tanh(ha + hb_next + c0)
        out = _dot(hid, w2_ref[...])
        n = lax.broadcasted_iota(jnp.int32, out.shape, 0)
        out_ref[...] = jnp.where(n < nrow - 1, out, 0.0)

    one(k2_ref, pek_ref, w1k_ref, w2k_ref, ko_ref)
    one(v2_ref, pev_ref, w1v_ref, w2v_ref, vo_ref)


def _compress(kc2, vc2, pe_k, pe_v, w1_k, w2_k, w1_v, w2_v):
    B, G, M, W = kc2.shape
    blk = pl.BlockSpec((None, None, M, W), lambda b, g: (b, g, 0, 0))
    const = lambda shp: pl.BlockSpec(shp, lambda b, g: (0,) * len(shp))
    flat = CMP_LEN * HEAD_DIM
    return pl.pallas_call(
        _cmp_body,
        grid=(B, G),
        in_specs=[blk, blk, const((1, flat)), const((1, flat)),
                  const((flat, CMP_HIDDEN)), const((CMP_HIDDEN, HEAD_DIM)),
                  const((flat, CMP_HIDDEN)), const((CMP_HIDDEN, HEAD_DIM))],
        out_specs=[pl.BlockSpec((None, None, M, HEAD_DIM), lambda b, g: (b, g, 0, 0))] * 2,
        out_shape=[jax.ShapeDtypeStruct((B, G, M, HEAD_DIM), f32)] * 2,
        compiler_params=_cparams("parallel", "parallel"),
        name="nsa_compress",
    )(kc2, vc2, pe_k.reshape(1, flat).astype(bf16), pe_v.reshape(1, flat).astype(bf16),
      w1_k.astype(bf16), w2_k.astype(bf16), w1_v.astype(bf16), w2_v.astype(bf16))


def _nsa_body(q_ref, kc_ref, vct_ref, ks_ref, vst_ref, kw_ref, vwt_ref, gl_ref, slope_ref, ovt_ref,
              o_ref, score_ref, sel_ref):
    QT = NSA_QT
    KT = NSA_KT
    R = NSA_Q_PER_KV
    Dh = HEAD_DIM
    T = ks_ref.shape[0]
    NB = T // SEL_LEN
    NC = kc_ref.shape[0]
    t0 = pl.program_id(2) * QT

    qt = q_ref[...]
    q3 = jnp.concatenate([qt[r * Dh:(r + 1) * Dh, :] for r in range(R)], axis=1)
    slope3 = slope_ref[...]
    tq1 = t0 + lax.broadcasted_iota(jnp.int32, (1, QT), 1)
    tq3 = jnp.concatenate([tq1] * R, axis=1)

    def softmax_cols(s, mask):
        s = jnp.where(mask, s, NEG_INF)
        m = jnp.max(s, axis=0, keepdims=True)
        p = jnp.where(mask, jnp.exp(s - m), 0.0)
        l = jnp.sum(p, axis=0, keepdims=True)
        return p * jnp.where(l > 0.0, 1.0 / l, 0.0)

    n_c = lax.broadcasted_iota(jnp.int32, (NC, 1), 0)
    d_c = tq3 - (n_c * CMP_STRIDE + (CMP_LEN - 1))
    mask_c = (d_c >= 0) & (n_c < NC - 1)
    s_c = jnp.dot(kc_ref[...], q3, preferred_element_type=f32) - slope3 * jnp.abs(d_c).astype(f32)
    p_c = softmax_cols(s_c, mask_c)
    o_c = jnp.dot(vct_ref[...], p_c.astype(bf16), preferred_element_type=f32)
    p_sum = p_c[:, 0:QT]
    for r in range(1, R):
        p_sum = p_sum + p_c[:, r * QT:(r + 1) * QT]
    imp = _dot_hi(ovt_ref[...], p_sum)

    j = lax.broadcasted_iota(jnp.int32, (NB, 1), 0)
    cur = tq1 >> (SEL_LEN.bit_length() - 1)
    valid = (j * SEL_LEN) <= tq1
    forced = (j == 0) | (j == cur) | (j == cur - 1)
    score = jnp.where(valid, imp + FORCED_BONUS * forced.astype(f32), -jnp.inf)
    score_ref[...] = score

    def rank_step(i, cnt):
        row = score_ref[pl.ds(i, 1), :]
        ahead = (row > score) | ((row == score) & (i < j))
        return cnt + jnp.where(ahead, 1.0, 0.0)

    cnt = lax.fori_loop(0, NB, rank_step, jnp.zeros((NB, QT), f32))
    sel_ref[...] = jnp.where((cnt < float(min(SEL_TOPK, NB))) & valid, 1.0, 0.0)

    bpt = KT // SEL_LEN
    pos_k = lax.broadcasted_iota(jnp.int32, (KT, 1), 0)

    def sel_step(kt, carry):
        m, l, acc = carry
        k0 = pl.multiple_of(kt * KT, KT)
        s = jnp.dot(ks_ref[pl.ds(k0, KT), :], q3, preferred_element_type=f32)
        d = tq3 - (k0 + pos_k)
        rows = []
        for jb in range(bpt):
            srow = sel_ref[pl.ds(kt * bpt + jb, 1), :]
            rows.append(jnp.broadcast_to(jnp.concatenate([srow] * R, axis=1), (SEL_LEN, R * QT)))
        mask = (jnp.concatenate(rows, axis=0) > 0.5) & (d >= 0)
        s = jnp.where(mask, s - slope3 * d.astype(f32), NEG_INF)
        m_new = jnp.maximum(m, jnp.max(s, axis=0, keepdims=True))
        alpha = jnp.exp(m - m_new)
        p = jnp.where(mask, jnp.exp(s - m_new), 0.0)
        l = alpha * l + jnp.sum(p, axis=0, keepdims=True)
        acc = alpha * acc + jnp.dot(vst_ref[:, pl.ds(k0, KT)], p.astype(bf16), preferred_element_type=f32)
        return m_new, l, acc

    n_kt = (t0 + QT + KT - 1) // KT
    init = (jnp.full((1, R * QT), NEG_INF, f32), jnp.zeros((1, R * QT), f32), jnp.zeros((Dh, R * QT), f32))
    _, l_s, acc_s = lax.fori_loop(0, n_kt, sel_step, init)
    o_s = acc_s * (1.0 / l_s)

    WL = WINDOW + QT
    w0 = pl.multiple_of(jnp.maximum(t0 - WINDOW, 0), LANES)
    pos_w = w0 + lax.broadcasted_iota(jnp.int32, (WL, 1), 0)
    d_w = tq3 - pos_w
    mask_w = (d_w >= 0) & (d_w < WINDOW)
    s_w = jnp.dot(kw_ref[pl.ds(w0, WL), :], q3, preferred_element_type=f32) - slope3 * jnp.abs(d_w).astype(f32)
    p_w = softmax_cols(s_w, mask_w)
    o_w = jnp.dot(vwt_ref[:, pl.ds(w0, WL)], p_w.astype(bf16), preferred_element_type=f32)

    gates = _sigmoid(gl_ref[...])
    for r in range(R):
        ls = slice(r * QT, (r + 1) * QT)
        o_ref[r * Dh:(r + 1) * Dh, :] = (gates[3 * r:3 * r + 1, :] * o_c[:, ls]
                                          + gates[3 * r + 1:3 * r + 2, :] * o_s[:, ls]
                                          + gates[3 * r + 2:3 * r + 3, :] * o_w[:, ls])


def _alibi_slopes(n):
    def pow2(m):
        start = 2.0 ** (-8.0 / m)
        return [start ** (i + 1) for i in range(m)]
    if math.log2(n).is_integer():
        return pow2(n)
    c = 2 ** math.floor(math.log2(n))
    return pow2(c) + pow2(2 * c)[0::2][: n - c]


def _nsa_attend(q_t, k_cmp, v_cmp_t, ks, vs_t, kw, vw_t, gl_t):
    B, G, T, Dh = ks.shape
    QT = NSA_QT
    R = NSA_Q_PER_KV
    NB = T // SEL_LEN
    NC = k_cmp.shape[2]
    slopes = np.asarray(_alibi_slopes(NSA_Q_HEADS), np.float32).reshape(G, R)
    slope3 = jnp.asarray(np.repeat(slopes, QT, axis=1).reshape(G, 1, R * QT))
    n = np.arange(NC)
    jb = np.arange(NB)
    ov = ((n[:, None] * CMP_STRIDE < jb[None, :] * SEL_LEN + SEL_LEN)
          & (n[:, None] * CMP_STRIDE + CMP_LEN - 1 >= jb[None, :] * SEL_LEN) & (n[:, None] < NC - 1))
    ovt = jnp.asarray(ov.T.astype(np.float32))
    per_bg = lambda shp: pl.BlockSpec((None, None) + shp, lambda b, g, i: (b, g, 0, 0))
    return pl.pallas_call(
        _nsa_body,
        grid=(B, G, T // QT),
        in_specs=[pl.BlockSpec((None, R * Dh, QT), lambda b, g, i: (b, g, i)),
                  per_bg((NC, Dh)), per_bg((Dh, NC)),
                  per_bg((T, Dh)), per_bg((Dh, T)), per_bg((T, Dh)), per_bg((Dh, T)),
                  pl.BlockSpec((None, None, R * NSA_N_BRANCH, QT), lambda b, g, i: (b, g, 0, i)),
                  pl.BlockSpec((None, 1, R * QT), lambda b, g, i: (g, 0, 0)),
                  pl.BlockSpec((NB, NC), lambda b, g, i: (0, 0))],
        out_specs=pl.BlockSpec((None, R * Dh, QT), lambda b, g, i: (b, g, i)),
        out_shape=jax.ShapeDtypeStruct((B, G * R * Dh, T), f32),
        scratch_shapes=[pltpu.VMEM((NB, QT), f32), pltpu.VMEM((NB, QT), f32)],
        compiler_params=_cparams("parallel", "parallel", "arbitrary"),
        name="nsa_attend",
    )(q_t, k_cmp, v_cmp_t, ks, vs_t, kw, vw_t, gl_t, slope3, ovt)


def _nsa(z_nsa, pe_k, pe_v, w1_k, w2_k, w1_v, w2_v):
    B, T, _ = z_nsa.shape
    G, Dh = NSA_KV_GROUPS, HEAD_DIM
    scale = HEAD_DIM ** -0.5
    q_t = jnp.swapaxes(z_nsa[:, :, 0:NSA_W] * scale, 1, 2).astype(bf16)

    def kv(i):
        o = NSA_W + i * NSA_KV_W
        return z_nsa[:, :, o:o + NSA_KV_W].reshape(B, T, G, Dh).astype(bf16)
    nat = lambda t: jnp.transpose(t, (0, 2, 1, 3))
    trn = lambda t: jnp.transpose(t, (0, 2, 3, 1))
    rows16 = lambda t: nat(t).reshape(B, G, T // CMP_STRIDE, CMP_STRIDE * Dh)
    k_cmp, v_cmp = _compress(rows16(kv(0)), rows16(kv(1)), pe_k, pe_v, w1_k, w2_k, w1_v, w2_v)
    gl_t = jnp.transpose(z_nsa[:, :, NSA_GATE_OFF:NSA_GATE_OFF + NSA_Q_HEADS * NSA_N_BRANCH]
                         .reshape(B, T, G, NSA_Q_PER_KV * NSA_N_BRANCH), (0, 2, 3, 1))
    o_t = _nsa_attend(q_t, k_cmp.astype(bf16), jnp.swapaxes(v_cmp, 2, 3).astype(bf16),
                      nat(kv(2)), trn(kv(3)), nat(kv(4)), trn(kv(5)), gl_t)
    return jnp.swapaxes(o_t, 1, 2)


def _s5_body(u_ref, tg_ref, bend_ref, cpow_ref, are_ref, aim_ref, y_ref, start_ref, inc_ref):
    nk = u_ref.shape[0] // SUBLANES
    u = u_ref[...]
    inc_ref[...] = jnp.dot(u, bend_ref[...], preferred_element_type=f32)
    are = are_ref[...]
    aim = aim_ref[...]

    def step(kc, st):
        r0 = pl.multiple_of(kc * SUBLANES, SUBLANES)
        start_ref[pl.ds(r0, SUBLANES), :] = st
        return are * st + aim * pltpu.roll(st, S5_STATE, 1) + inc_ref[pl.ds(r0, SUBLANES), :]

    lax.fori_loop(0, nk, step, jnp.zeros((SUBLANES, 2 * S5_STATE), f32))
    y_ref[...] = (jnp.dot(u, tg_ref[...], preferred_element_type=f32)
                  + jnp.dot(start_ref[...].astype(bf16), cpow_ref[...], preferred_element_type=f32))


def _s5_tables(lam_re, lam_im, log_dt, b_re, b_im, c_re, c_im):
    Cs = S5_CHUNK
    lam_re, lam_im = lam_re.astype(f32), lam_im.astype(f32)
    b_re, b_im, c_re, c_im = (t.astype(f32) for t in (b_re, b_im, c_re, c_im))
    dt = jnp.exp(log_dt.astype(f32))[:, None]
    mag = jnp.exp(lam_re * dt)
    ab_re, ab_im = mag * jnp.cos(lam_im * dt), mag * jnp.sin(lam_im * dt)
    den = lam_re * lam_re + lam_im * lam_im
    f_re = ((ab_re - 1.0) * lam_re + ab_im * lam_im) / den
    f_im = (ab_im * lam_re - (ab_re - 1.0) * lam_im) / den
    bb_re = f_re[..., None] * b_re - f_im[..., None] * b_im
    bb_im = f_re[..., None] * b_im + f_im[..., None] * b_re
    tau = jnp.arange(Cs + 1, dtype=f32)[None, None, :]
    pmag = jnp.exp(lam_re[..., None] * dt[..., None] * tau)
    pang = lam_im[..., None] * dt[..., None] * tau
    pw_re, pw_im = pmag * jnp.cos(pang), pmag * jnp.sin(pang)
    ein = functools.partial(jnp.einsum, precision=HI)
    ca_re = c_re[..., None] * pw_re[:, None] - c_im[..., None] * pw_im[:, None]
    ca_im = c_re[..., None] * pw_im[:, None] + c_im[..., None] * pw_re[:, None]
    kern = ein('gopt,gpi->gtoi', ca_re[..., :Cs], bb_re) - ein('gopt,gpi->gtoi', ca_im[..., :Cs], bb_im)
    s_idx = jnp.arange(Cs)
    lag = s_idx[None, :] - s_idx[:, None]
    tg = jnp.where((lag >= 0)[None, :, :, None, None], kern[:, jnp.clip(lag, 0, Cs - 1)], 0.0)
    tg = jnp.transpose(tg, (0, 1, 4, 2, 3)).reshape(S5_GROUPS, Cs * S5_CH, Cs * S5_CH)
    rev = (Cs - 1) - s_idx
    pe_re, pe_im = pw_re[:, :, rev], pw_im[:, :, rev]
    be_re = pe_re[..., None] * bb_re[:, :, None, :] - pe_im[..., None] * bb_im[:, :, None, :]
    be_im = pe_re[..., None] * bb_im[:, :, None, :] + pe_im[..., None] * bb_re[:, :, None, :]
    bend = jnp.concatenate([jnp.transpose(be_re, (0, 2, 3, 1)), jnp.transpose(be_im, (0, 2, 3, 1))], axis=-1)
    bend = bend.reshape(S5_GROUPS, Cs * S5_CH, 2 * S5_STATE)
    cp_re = jnp.transpose(ca_re[..., 1:], (0, 2, 3, 1))
    cp_im = jnp.transpose(ca_im[..., 1:], (0, 2, 3, 1))
    cpow = jnp.concatenate([cp_re, -cp_im], axis=1).reshape(S5_GROUPS, 2 * S5_STATE, Cs * S5_CH)
    a_re = jnp.concatenate([pw_re[..., Cs], pw_re[..., Cs]], axis=-1)[:, None, :]
    a_im = jnp.concatenate([-pw_im[..., Cs], pw_im[..., Cs]], axis=-1)[:, None, :]
    return tg.astype(bf16), bend.astype(bf16), cpow.astype(bf16), a_re, a_im


def _s5_scan(z_s5, tables):
    B, T, _ = z_s5.shape
    assert B == SUBLANES
    Cs = S5_CHUNK
    nk = T // Cs
    tg, bend, cpow, a_re, a_im = tables
    u = z_s5.reshape(B, nk, Cs, S5_GROUPS, S5_CH)
    u = jnp.transpose(u, (3, 1, 0, 2, 4)).reshape(S5_GROUPS, nk * B, Cs * S5_CH).astype(bf16)
    W = Cs * S5_CH
    per_g = lambda shp: pl.BlockSpec((None,) + shp, lambda g: (g, 0, 0))
    y = pl.pallas_call(
        _s5_body,
        grid=(S5_GROUPS,),
        in_specs=[per_g((nk * B, W)), per_g((W, W)), per_g((W, 2 * S5_STATE)), per_g((2 * S5_STATE, W)),
                  per_g((1, 2 * S5_STATE)), per_g((1, 2 * S5_STATE))],
        out_specs=per_g((nk * B, W)),
        out_shape=jax.ShapeDtypeStruct((S5_GROUPS, nk * B, W), f32),
        scratch_shapes=[pltpu.VMEM((nk * B, 2 * S5_STATE), f32), pltpu.VMEM((nk * B, 2 * S5_STATE), f32)],
        compiler_params=_cparams("parallel"),
        name="s5_conv",
    )(u, tg, bend, cpow, a_re, a_im)
    y = y.reshape(S5_GROUPS, nk, B, Cs, S5_CH)
    return jnp.transpose(y, (2, 1, 3, 0, 4)).reshape(B, T, S5_W)


def _s5_post_body(y_ref, u_ref, d_ref, w_ref, o_ref):
    y = _gelu_tanh(y_ref[...] + d_ref[...] * u_ref[...])
    vg = _dot(y, w_ref[...])
    o_ref[...] = vg[:, 0:S5_W] * _sigmoid(vg[:, S5_W:2 * S5_W])


def _s5_post(y2d, u2d, d_skip, w_glu):
    n = y2d.shape[0]
    tm = min(S5P_TM, n)
    blk = pl.BlockSpec((tm, S5_W), lambda i: (i, 0))
    return pl.pallas_call(
        _s5_post_body,
        grid=(n // tm,),
        in_specs=[blk, blk, pl.BlockSpec((1, S5_W), lambda i: (0, 0)),
                  pl.BlockSpec((S5_W, 2 * S5_W), lambda i: (0, 0))],
        out_specs=blk,
        out_shape=jax.ShapeDtypeStruct((n, S5_W), f32),
        compiler_params=_cparams("parallel"),
        name="s5_glu",
    )(y2d, u2d, d_skip.reshape(1, S5_W), w_glu.astype(bf16))


def _out_proj_body(h_ref, orw_ref, onsa_ref, os5_ref, w_ref, g_ref, o_ref):
    mix = (_dot(orw_ref[...], w_ref[0:RWKV_W, :])
           + _dot(onsa_ref[...], w_ref[RWKV_W:RWKV_W + NSA_W, :])
           + _dot(os5_ref[...], w_ref[RWKV_W + NSA_W:D_MIX, :]))
    o_ref[...] = h_ref[...] + _rms(mix, g_ref[...])


def _out_proj(h2d, o_rw, o_nsa, o_s5, w_out, g):
    n = h2d.shape[0]
    tm = min(OUT_TM, n)
    rows = lambda w: pl.BlockSpec((tm, w), lambda i: (i, 0))
    return pl.pallas_call(
        _out_proj_body,
        grid=(n // tm,),
        in_specs=[rows(D_MODEL), rows(RWKV_W), rows(NSA_W), rows(S5_W),
                  pl.BlockSpec((D_MIX, D_MODEL), lambda i: (0, 0)),
                  pl.BlockSpec((1, D_MODEL), lambda i: (0, 0))],
        out_specs=rows(D_MODEL),
        out_shape=jax.ShapeDtypeStruct((n, D_MODEL), f32),
        compiler_params=_cparams("parallel"),
        name="out_proj",
    )(h2d, o_rw, o_nsa, o_s5, w_out.astype(bf16), g.reshape(1, D_MODEL))


def _ffn_body(h_ref, hp_ref, gpre_ref, wg_ref, wu_ref, cwg_ref, cwu_ref, cbg_ref, cbu_ref, wd_ref, gpost_ref,
              o_ref, xn_ref, acc_ref, *, tiles_per_seq):
    tm = h_ref.shape[0]
    H = SUBLANES
    c = pl.program_id(1)

    @pl.when(c == 0)
    def _():
        first = (pl.program_id(0) % tiles_per_seq) == 0
        xp = _rms(hp_ref[...], gpre_ref[...])
        xn_ref[0:H, :] = jnp.where(first, 0.0, xp).astype(bf16)
        xn_ref[H:H + tm, :] = _rms(h_ref[...], gpre_ref[...]).astype(bf16)
        acc_ref[...] = jnp.zeros_like(acc_ref)

    xn = xn_ref[...]

    def conv_branch(w_ref, cw_ref, cb_ref):
        hu = jnp.dot(xn, w_ref[...], preferred_element_type=f32)
        cw = cw_ref[...]
        out = (cw[0:1, :] * pltpu.roll(hu, 2, 0)[H:H + tm]
               + cw[1:2, :] * pltpu.roll(hu, 1, 0)[H:H + tm]
               + cw[2:3, :] * hu[H:H + tm])
        return out + cb_ref[...]

    gate = conv_branch(wg_ref, cwg_ref, cbg_ref)
    up = conv_branch(wu_ref, cwu_ref, cbu_ref)
    acc_ref[...] += _dot(_gelu_tanh(gate) * up, wd_ref[...])

    @pl.when(c == pl.num_programs(1) - 1)
    def _():
        o_ref[...] = h_ref[...] + _rms(acc_ref[...], gpost_ref[...])


def _conv_ffn(h2d, seq_len, g_pre, w_up, conv_w, conv_b, w_down, g_post):
    n = h2d.shape[0]
    tm = min(FFN_TM, seq_len)
    tc = FFN_TC
    nc = D_FF // tc
    H = SUBLANES
    tiles_per_seq = seq_len // tm
    hpb = tm // H
    w_up = w_up.astype(bf16)
    cb = conv_b.reshape(1, 2 * D_FF)
    return pl.pallas_call(
        functools.partial(_ffn_body, tiles_per_seq=tiles_per_seq),
        grid=(n // tm, nc),
        in_specs=[pl.BlockSpec((tm, D_MODEL), lambda i, c: (i, 0)),
                  pl.BlockSpec((H, D_MODEL), lambda i, c: (jnp.maximum(i * hpb - 1, 0), 0)),
                  pl.BlockSpec((1, D_MODEL), lambda i, c: (0, 0)),
                  pl.BlockSpec((D_MODEL, tc), lambda i, c: (0, c)),
                  pl.BlockSpec((D_MODEL, tc), lambda i, c: (0, nc + c)),
                  pl.BlockSpec((CONV_W, tc), lambda i, c: (0, c)),
                  pl.BlockSpec((CONV_W, tc), lambda i, c: (0, nc + c)),
                  pl.BlockSpec((1, tc), lambda i, c: (0, c)),
                  pl.BlockSpec((1, tc), lambda i, c: (0, nc + c)),
                  pl.BlockSpec((tc, D_MODEL), lambda i, c: (c, 0)),
                  pl.BlockSpec((1, D_MODEL), lambda i, c: (0, 0))],
        out_specs=pl.BlockSpec((tm, D_MODEL), lambda i, c: (i, 0)),
        out_shape=jax.ShapeDtypeStruct((n, D_MODEL), f32),
        scratch_shapes=[pltpu.VMEM((H + tm, D_MODEL), bf16), pltpu.VMEM((tm, D_MODEL), f32)],
        compiler_params=_cparams("parallel", "arbitrary"),
        name="conv_ffn",
    )(h2d, h2d, g_pre.reshape(1, D_MODEL), w_up, w_up, conv_w, conv_w, cb, cb, w_down.astype(bf16),
      g_post.reshape(1, D_MODEL))


def _ple_body(h_ref, p_ref, wp_ref, gp_ref, wg_ref, o_ref):
    h = h_ref[...]
    e = _rms(_dot(p_ref[...], wp_ref[...]), gp_ref[...])
    gate = _sigmoid(_dot(h, wg_ref[...]))
    o_ref[...] = h + gate * e


def _ple(h2d, p2d, w_ple, g_ple, w_gate):
    n = h2d.shape[0]
    tm = min(PLE_TM, n)
    return pl.pallas_call(
        _ple_body,
        grid=(n // tm,),
        in_specs=[pl.BlockSpec((tm, D_MODEL), lambda i: (i, 0)),
                  pl.BlockSpec((tm, PLE_DIM), lambda i: (i, 0)),
                  pl.BlockSpec((PLE_DIM, D_MODEL), lambda i: (0, 0)),
                  pl.BlockSpec((1, D_MODEL), lambda i: (0, 0)),
                  pl.BlockSpec((D_MODEL, D_MODEL), lambda i: (0, 0))],
        out_specs=pl.BlockSpec((tm, D_MODEL), lambda i: (i, 0)),
        out_shape=jax.ShapeDtypeStruct((n, D_MODEL), f32),
        compiler_params=_cparams("parallel"),
        name="ple_gate",
    )(h2d, p2d, w_ple.astype(bf16), g_ple.reshape(1, D_MODEL), w_gate.astype(bf16))


def kernel(x, p, pre_mix_norm, post_mix_norm, pre_ffn_norm, post_ffn_norm, w_in, w_out, shift_mu, rw_w0, rw_w2, rw_a0, rw_a2, rw_g2, rw_k_k, rw_k_a, rw_r_k, rw_gn_w, rw_gn_b, cmp_pe_k, cmp_pe_v, cmp_w1_k, cmp_w2_k, cmp_w1_v, cmp_w2_v, s5_lam_re, s5_lam_im, s5_log_dt, s5_b_re, s5_b_im, s5_c_re, s5_c_im, s5_d, s5_w_glu, w_up, conv_w, conv_b, w_down, w_ple, ple_norm, w_ple_gate):
    B, T, D = x.shape
    n = B * T
    depth = w_in.shape[0]
    h = x.reshape(n, D)
    for i in range(depth):
        w_p, mu_p = _pad_in_weights(w_in[i], shift_mu[i])
        z_rw, z_nsa, z_s5 = _in_proj(h, pre_mix_norm[i], w_p)
        o_rw = _rwkv(z_rw.reshape(B, T, RW_P), mu_p, rw_w0[i], rw_w2[i], rw_a0[i], rw_a2[i], rw_g2[i],
                     rw_k_k[i], rw_k_a[i], rw_r_k[i], rw_gn_w[i], rw_gn_b[i])
        o_nsa = _nsa(z_nsa.reshape(B, T, NSA_P), cmp_pe_k[i], cmp_pe_v[i], cmp_w1_k[i], cmp_w2_k[i],
                     cmp_w1_v[i], cmp_w2_v[i])
        tables = _s5_tables(s5_lam_re[i], s5_lam_im[i], s5_log_dt[i], s5_b_re[i], s5_b_im[i], s5_c_re[i], s5_c_im[i])
        y_s5 = _s5_scan(z_s5.reshape(B, T, S5_W), tables)
        o_s5 = _s5_post(y_s5.reshape(n, S5_W), z_s5, s5_d[i], s5_w_glu[i])
        h = _out_proj(h, o_rw.reshape(n, RWKV_W), o_nsa.reshape(n, NSA_W), o_s5, w_out[i], post_mix_norm[i])
        h = _conv_ffn(h, T, pre_ffn_norm[i], w_up[i], conv_w[i], conv_b[i], w_down[i], post_ffn_norm[i])
        h = _ple(h, p[i].reshape(n, PLE_DIM), w_ple[i], ple_norm[i], w_ple_gate[i])
    return h.reshape(B, T, D)
```

```python
import functools
import math

import numpy as np
import jax
import jax.numpy as jnp
from jax import lax
from jax.experimental import pallas as pl
from jax.experimental.pallas import tpu as pltpu

f32 = jnp.float32
bf16 = jnp.bfloat16
HI = lax.Precision.HIGHEST

D_MODEL = 1024
HEAD_DIM = 64
RWKV_HEADS = 6
RWKV_W = RWKV_HEADS * HEAD_DIM
D_DECAY_LORA = 64
D_AAA_LORA = 64
D_GATE_LORA = 128
RWKV_COLS = 3 * RWKV_W + D_DECAY_LORA + D_AAA_LORA + D_GATE_LORA
GN_EPS = 64e-5
NSA_Q_HEADS = 6
NSA_KV_GROUPS = 2
NSA_Q_PER_KV = NSA_Q_HEADS // NSA_KV_GROUPS
NSA_W = NSA_Q_HEADS * HEAD_DIM
NSA_KV_W = NSA_KV_GROUPS * HEAD_DIM
NSA_N_BRANCH = 3
NSA_COLS = NSA_W + 6 * NSA_KV_W + NSA_Q_HEADS * NSA_N_BRANCH
CMP_LEN = 32
CMP_STRIDE = 16
CMP_HIDDEN = 128
SEL_LEN = 64
SEL_TOPK = 16
WINDOW = 512
FORCED_BONUS = 1e3
NEG_INF = -1e30
S5_GROUPS = 16
S5_CH = 16
S5_W = S5_GROUPS * S5_CH
S5_STATE = 64
D_MIX = RWKV_W + NSA_W + S5_W
D_FF = 2816
CONV_W = 3
PLE_DIM = 256
NORM_EPS = 1e-6

LANES = 128
SUBLANES = 8
VMEM_LIMIT = 56 * 1024 * 1024

RW_LORA_W_OFF = 3 * RWKV_W
RW_LORA_A_OFF = RW_LORA_W_OFF + LANES
RW_LORA_G_OFF = RW_LORA_A_OFF + LANES
RW_P = RW_LORA_G_OFF + D_GATE_LORA
NSA_GATE_OFF = NSA_W + 6 * NSA_KV_W
NSA_P = NSA_GATE_OFF + LANES
IN_P = RW_P + NSA_P + S5_W

IN_TM = 256
RW_CHUNK = 64
RW_NCH = 4
NSA_QT = 512
NSA_KB = 256
NSA_AUG = 16
S5_CHUNK = 64
OUT_TM = 512
FFN_TM = 1024
FFN_TC = 256
PLE_TM = 512
S5P_TM = 1024


def _cparams(*sem):
    return pltpu.CompilerParams(dimension_semantics=sem, vmem_limit_bytes=VMEM_LIMIT)


def _rms(x, g):
    ms = jnp.mean(x * x, axis=-1, keepdims=True)
    return x * lax.rsqrt(ms + NORM_EPS) * g


def _gelu_tanh(x):
    return 0.5 * x * (1.0 + jnp.tanh(math.sqrt(2.0 / math.pi) * (x + 0.044715 * (x * x * x))))


def _sigmoid(x):
    return 1.0 / (1.0 + jnp.exp(-x))


def _softplus(x):
    return jnp.maximum(x, 0.0) + jnp.log(1.0 + jnp.exp(-jnp.abs(x)))


def _dot(a, b):
    return jnp.dot(a.astype(bf16), b.astype(bf16), preferred_element_type=f32)


def _dot_nt(a, b):
    return lax.dot_general(a.astype(bf16), b.astype(bf16), (((1,), (1,)), ((), ())), preferred_element_type=f32)


def _dot_tn(a, b):
    return lax.dot_general(a.astype(bf16), b.astype(bf16), (((0,), (0,)), ((), ())), preferred_element_type=f32)


def _dot_hi(a, b):
    return jnp.dot(a, b, preferred_element_type=f32, precision=HI)


def _split_dot(x, m01, terms, m_left):
    m = m01.astype(bf16)
    acc = None
    rem = x
    for _ in range(terms):
        piece = rem.astype(bf16)
        d = (jnp.dot(m, piece, preferred_element_type=f32) if m_left
             else jnp.dot(piece, m, preferred_element_type=f32))
        acc = d if acc is None else acc + d
        rem = rem - piece.astype(f32)
    return acc


def _in_proj_body(x_ref, g_ref, w_ref, zrw_ref, znsa_ref, zs5_ref):
    xn = _rms(x_ref[...], g_ref[...]).astype(bf16)
    zrw_ref[...] = jnp.dot(xn, w_ref[:, 0:RW_P], preferred_element_type=f32)
    znsa_ref[...] = jnp.dot(xn, w_ref[:, RW_P:RW_P + NSA_P], preferred_element_type=f32)
    zs5_ref[...] = jnp.dot(xn, w_ref[:, RW_P + NSA_P:IN_P], preferred_element_type=f32)


def _in_proj(h2d, g, w_p):
    n = h2d.shape[0]
    tm = min(IN_TM, n)
    return pl.pallas_call(
        _in_proj_body,
        grid=(n // tm,),
        in_specs=[pl.BlockSpec((tm, D_MODEL), lambda i: (i, 0)),
                  pl.BlockSpec((1, D_MODEL), lambda i: (0, 0)),
                  pl.BlockSpec((D_MODEL, IN_P), lambda i: (0, 0))],
        out_specs=[pl.BlockSpec((tm, RW_P), lambda i: (i, 0)),
                   pl.BlockSpec((tm, NSA_P), lambda i: (i, 0)),
                   pl.BlockSpec((tm, S5_W), lambda i: (i, 0))],
        out_shape=[jax.ShapeDtypeStruct((n, RW_P), f32),
                   jax.ShapeDtypeStruct((n, NSA_P), f32),
                   jax.ShapeDtypeStruct((n, S5_W), f32)],
        compiler_params=_cparams("parallel"),
        name="in_proj",
    )(h2d, g.reshape(1, D_MODEL), w_p)


def _pad_in_weights(w_in, shift_mu):
    d = w_in.shape[0]
    zc = lambda k: jnp.zeros((d, k), w_in.dtype)
    o = 3 * RWKV_W
    rw = [w_in[:, :o],
          w_in[:, o:o + D_DECAY_LORA], zc(LANES - D_DECAY_LORA),
          w_in[:, o + D_DECAY_LORA:o + D_DECAY_LORA + D_AAA_LORA], zc(LANES - D_AAA_LORA),
          w_in[:, o + D_DECAY_LORA + D_AAA_LORA:RWKV_COLS]]
    nsa = [w_in[:, RWKV_COLS:RWKV_COLS + NSA_COLS], zc(NSA_P - NSA_COLS)]
    s5 = [w_in[:, RWKV_COLS + NSA_COLS:]]
    w_p = jnp.concatenate(rw + nsa + s5, axis=1).astype(bf16)
    z1 = lambda k: jnp.zeros((k,), shift_mu.dtype)
    mu_p = jnp.concatenate([shift_mu[:o],
                            shift_mu[o:o + D_DECAY_LORA], z1(LANES - D_DECAY_LORA),
                            shift_mu[o + D_DECAY_LORA:o + D_DECAY_LORA + D_AAA_LORA], z1(LANES - D_AAA_LORA),
                            shift_mu[o + D_DECAY_LORA + D_AAA_LORA:]])
    return w_p, mu_p.reshape(1, RW_P)


def _rwkv_body(z_ref, mu_ref, w0_ref, w2_ref, a0_ref, a2_ref, g2_ref, kk_ref, ka_ref, rk_ref, gnw_ref, gnb_ref,
               o_ref, state_ref, prev_ref):
    C = RW_CHUNK
    N = HEAD_DIM
    L = z_ref.shape[0]
    nch = L // C
    npair = RWKV_W // LANES

    @pl.when(pl.program_id(1) == 0)
    def _():
        state_ref[...] = jnp.zeros_like(state_ref)
        prev_ref[...] = jnp.zeros_like(prev_ref)

    z = z_ref[...]
    row = lax.broadcasted_iota(jnp.int32, z.shape, 0)
    zprev = jnp.where(row == 0, prev_ref[0:1, :], pltpu.roll(z, 1, 0))
    prev_ref[0:1, :] = z[L - 1:L, :]
    zs = z + (zprev - z) * mu_ref[...]

    r = zs[:, 0:RWKV_W]
    k = zs[:, RWKV_W:2 * RWKV_W]
    v = zs[:, 2 * RWKV_W:3 * RWKV_W]
    wl = zs[:, RW_LORA_W_OFF:RW_LORA_W_OFF + LANES]
    al = zs[:, RW_LORA_A_OFF:RW_LORA_A_OFF + LANES]
    gl = zs[:, RW_LORA_G_OFF:RW_LORA_G_OFF + D_GATE_LORA]
    lane = lax.broadcasted_iota(jnp.int32, (1, LANES), 1)
    li = lax.broadcasted_iota(jnp.int32, (LANES, LANES), 0)
    lj = lax.broadcasted_iota(jnp.int32, (LANES, LANES), 1)
    hshift = N.bit_length() - 1
    bd128 = ((li >> hshift) == (lj >> hshift)).astype(f32)
    eye128 = (li == lj).astype(f32)
    first_half = lane < N
    lane3 = lax.broadcasted_iota(jnp.int32, (1, RWKV_W), 1)
    half_mask = [(((lane3 >> hshift) & 1) == hh).astype(f32) for hh in range(2)]

    def head_sums(x):
        return jnp.concatenate([_split_dot(x[:, j * LANES:(j + 1) * LANES], bd128, 2, False)
                                for j in range(npair)], axis=1)

    w = -_softplus(-(w0_ref[...] + _dot(jnp.tanh(wl), w2_ref[...]))) - 0.5
    logd = -jnp.exp(w)
    a = _sigmoid(a0_ref[...] + _dot(al, a2_ref[...]))
    g = _dot(_sigmoid(gl), g2_ref[...])
    kkr = k * kk_ref[...]
    kk = kkr / jnp.maximum(jnp.sqrt(head_sums(kkr * kkr)), 1e-12)
    kmod = k * (1.0 + (a - 1.0) * ka_ref[...])
    b = kk * a

    ti = lax.broadcasted_iota(jnp.int32, (C, C), 0)
    si = lax.broadcasted_iota(jnp.int32, (C, C), 1)
    tri_incl = (si <= ti).astype(f32)
    tri_strict = (si < ti).astype(f32)
    eye = (si == ti).astype(f32)
    cums, tots = [], []
    for c in range(nch):
        cc = _split_dot(logd[c * C:(c + 1) * C], tri_incl, 3, True)
        cums.append(cc)
        tots.append(jnp.broadcast_to(cc[C - 1:C], (C, RWKV_W)))
    cum = jnp.concatenate(cums, axis=0)
    tot = jnp.concatenate(tots, axis=0)
    e_inv = jnp.exp(-cum)
    e_end = jnp.exp(tot - cum)
    r_t = r * jnp.exp(cum)
    kk_t = kk * jnp.exp(cum - logd)
    k_h = kmod * e_inv
    b_h = b * e_inv
    k_e = kmod * e_end
    b_e = b * e_end
    e_tot = jnp.exp(tot)
    kk_t_m = [kk_t * m for m in half_mask]
    r_t_m = [r_t * m for m in half_mask]
    k_e_m = [k_e * m for m in half_mask]
    b_e_m = [b_e * m for m in half_mask]

    lvl_masks = []
    s = 1
    while s < C:
        sh = s.bit_length() - 1
        m = ((ti >> (sh + 1)) == (si >> (sh + 1))) & (((ti >> sh) & 1) == 1) & (((si >> sh) & 1) == 0)
        lvl_masks.append(m.astype(f32))
        s *= 2

    chains = [(c, h) for c in range(nch) for h in range(RWKV_HEADS)]

    def blk(arr, c, h):
        j = h // 2
        return arr[c * C:(c + 1) * C, j * LANES:(j + 1) * LANES]

    p = [_dot_nt(jnp.concatenate([blk(kk_t_m[h % 2], c, h), blk(r_t_m[h % 2], c, h)], axis=0),
                 jnp.concatenate([blk(b_h, c, h), blk(k_h, c, h)], axis=0)) for c, h in chains]
    a_ab = [t[0:C, 0:C] * tri_strict for t in p]
    a_ak = [t[0:C, C:2 * C] * tri_strict for t in p]
    a_rb = [t[C:2 * C, 0:C] * tri_incl for t in p]
    a_rk = [t[C:2 * C, C:2 * C] * tri_incl for t in p]
    x = [eye - t * lvl_masks[0] for t in a_ab]
    for m in lvl_masks[1:]:
        t1 = [_dot(xg, ag * m) for xg, ag in zip(x, a_ab)]
        x = [xg - _dot(tg, xg) for xg, tg in zip(x, t1)]
    av = [_dot(jnp.concatenate([a_ak[i], a_rk[i]], axis=0), blk(v, c, h)) for i, (c, h) in enumerate(chains)]
    wu = [_dot(x[i], jnp.concatenate([blk(kk_t, c, h), av[i][0:C]], axis=1)) for i, (c, h) in enumerate(chains)]
    bwu = [_dot_tn(blk(b_e_m[h % 2], c, h), wu[i]) for i, (c, h) in enumerate(chains)]
    ktv = [_dot_tn(blk(k_e_m[h % 2], c, h), blk(v, c, h)) for c, h in chains]
    arb = [_dot(a_rb[i], wu[i]) for i in range(len(chains))]

    state = [state_ref[j] for j in range(npair)]
    out_rows = []
    for c in range(nch):
        out_lanes = []
        for j in range(npair):
            ge = c * RWKV_HEADS + 2 * j
            go = ge + 1
            bsum = bwu[ge] + bwu[go]
            tm = eye128 * e_tot[c * C:c * C + 1, j * LANES:(j + 1) * LANES] - bsum[:, 0:LANES] * bd128
            zc = (ktv[ge] + ktv[go] - bsum[:, LANES:2 * LANES]) * bd128
            q = blk(r_t, c, 2 * j) - jnp.where(first_half, arb[ge][:, 0:LANES], arb[go][:, 0:LANES])
            o_in = jnp.where(first_half, av[ge][C:2 * C] - arb[ge][:, LANES:2 * LANES],
                             av[go][C:2 * C] - arb[go][:, LANES:2 * LANES])
            ts = _dot(jnp.concatenate([tm, q], axis=0), state[j])
            state[j] = ts[0:LANES] + zc
            out_lanes.append(ts[LANES:LANES + C] + o_in)
        out_rows.append(jnp.concatenate(out_lanes, axis=1))
    for j in range(npair):
        state_ref[j] = state[j]
    o = jnp.concatenate(out_rows, axis=0)

    inv_n = 1.0 / N
    mean = head_sums(o) * inv_n
    oc = o - mean
    var = head_sums(oc * oc) * inv_n
    on = oc * lax.rsqrt(var + GN_EPS) * gnw_ref[...] + gnb_ref[...]
    bonus = head_sums(r * kmod * rk_ref[...]) * v
    o_ref[...] = (on + bonus) * g


def _rwkv(z_rw, mu_p, w0, w2, a0, a2, g2, k_k, k_a, r_k, gn_w, gn_b):
    B, T, _ = z_rw.shape
    L = RW_CHUNK * RW_NCH
    row = lambda t: t.reshape(1, RWKV_W)
    w2p = jnp.concatenate([w2, jnp.zeros((LANES - D_DECAY_LORA, RWKV_W), w2.dtype)], axis=0).astype(bf16)
    a2p = jnp.concatenate([a2, jnp.zeros((LANES - D_AAA_LORA, RWKV_W), a2.dtype)], axis=0).astype(bf16)
    const = lambda shp: pl.BlockSpec(shp, lambda b, c: (0,) * len(shp))
    return pl.pallas_call(
        _rwkv_body,
        grid=(B, T // L),
        in_specs=[pl.BlockSpec((None, L, RW_P), lambda b, c: (b, c, 0)),
                  const((1, RW_P)), const((1, RWKV_W)), const((LANES, RWKV_W)), const((1, RWKV_W)),
                  const((LANES, RWKV_W)), const((D_GATE_LORA, RWKV_W)), const((1, RWKV_W)), const((1, RWKV_W)),
                  const((1, RWKV_W)), const((1, RWKV_W)), const((1, RWKV_W))],
        out_specs=pl.BlockSpec((None, L, RWKV_W), lambda b, c: (b, c, 0)),
        out_shape=jax.ShapeDtypeStruct((B, T, RWKV_W), f32),
        scratch_shapes=[pltpu.VMEM((RWKV_W // LANES, LANES, LANES), f32),
                        pltpu.VMEM((SUBLANES, RW_P), f32)],
        compiler_params=_cparams("parallel", "arbitrary"),
        name="rwkv7",
    )(z_rw, mu_p, row(w0), w2p, row(a0), a2p, g2.astype(bf16), row(k_k), row(k_a), row(r_k), row(gn_w), row(gn_b))


def _cmp_body(k2_ref, v2_ref, pek_ref, pev_ref, w1k_ref, w2k_ref, w1v_ref, w2v_ref, ko_ref, vo_ref):
    half = (CMP_LEN // 2) * HEAD_DIM

    def one(x2_ref, pe_ref, w1_ref, w2_ref, out_ref):
        x2 = x2_ref[...]
        nrow = x2.shape[0]
        ha = jnp.dot(x2, w1_ref[0:half, :], preferred_element_type=f32)
        hb = jnp.dot(x2, w1_ref[half:2 * half, :], preferred_element_type=f32)
        hb_next = pltpu.roll(hb, nrow - 1, 0)
        c0 = jnp.dot(pe_ref[...], w1_ref[...], preferred_element_type=f32)
        hid = _gelu_tanh(ha + hb_next + c0)
        out = _dot(hid, w2_ref[...])
        n = lax.broadcasted_iota(jnp.int32, out.shape, 0)
        out_ref[...] = jnp.where(n < nrow - 1, out, 0.0)

    one(k2_ref, pek_ref, w1k_ref, w2k_ref, ko_ref)
    one(v2_ref, pev_ref, w1v_ref, w2v_ref, vo_ref)


def _compress(kc2, vc2, pe_k, pe_v, w1_k, w2_k, w1_v, w2_v):
    B, G, M, W = kc2.shape
    blk = pl.BlockSpec((None, None, M, W), lambda b, g: (b, g, 0, 0))
    const = lambda shp: pl.BlockSpec(shp, lambda b, g: (0,) * len(shp))
    flat = CMP_LEN * HEAD_DIM
    return pl.pallas_call(
        _cmp_body,
        grid=(B, G),
        in_specs=[blk, blk, const((1, flat)), const((1, flat)),
                  const((flat, CMP_HIDDEN)), const((CMP_HIDDEN, HEAD_DIM)),
                  const((flat, CMP_HIDDEN)), const((CMP_HIDDEN, HEAD_DIM))],
        out_specs=[pl.BlockSpec((None, None, M, HEAD_DIM), lambda b, g: (b, g, 0, 0))] * 2,
        out_shape=[jax.ShapeDtypeStruct((B, G, M, HEAD_DIM), f32)] * 2,
        compiler_params=_cparams("parallel", "parallel"),
        name="nsa_compress",
    )(kc2, vc2, pe_k.reshape(1, flat).astype(bf16), pe_v.reshape(1, flat).astype(bf16),
      w1_k.astype(bf16), w2_k.astype(bf16), w1_v.astype(bf16), w2_v.astype(bf16))


def _nsa_body(q_ref, kc_ref, vct_ref, ks_ref, vst_ref, kw_ref, vwt_ref, gl_ref, ovt_ref,
              o_ref, score_ref, cnt_ref, q3_ref, selg_ref, s_sel, p_sel, acc_sel, s_win, p_win, acc_win):
    QT = NSA_QT
    KB = NSA_KB
    R = NSA_Q_PER_KV
    Dh = HEAD_DIM
    T = ks_ref.shape[0]
    NB = T // SEL_LEN
    NC = kc_ref.shape[0]
    qi = pl.program_id(2)
    t0 = qi * QT

    W = R * QT
    q3_ref[...] = jnp.concatenate([q_ref[r * LANES:(r + 1) * LANES, :] for r in range(R)], axis=1)
    slope_rows = q3_ref[Dh:Dh + NSA_AUG, :].astype(f32)[0:4]
    tq1 = t0 + lax.broadcasted_iota(jnp.int32, (1, QT), 1)
    tile3 = lambda t: jnp.concatenate([t] * R, axis=1)

    def col_reduce(x, op):
        parts = [x[i * SUBLANES:(i + 1) * SUBLANES] for i in range(x.shape[0] // SUBLANES)]
        while len(parts) > 1:
            nxt = [op(parts[2 * i], parts[2 * i + 1]) for i in range(len(parts) // 2)]
            parts = nxt + parts[2 * (len(parts) // 2):]
        return parts[0]

    colmax = lambda x: jnp.max(col_reduce(x, jnp.maximum), axis=0, keepdims=True)
    colsum = lambda x: jnp.sum(col_reduce(x, jnp.add), axis=0, keepdims=True)

    n_c = lax.broadcasted_iota(jnp.int32, (NC, 1), 0)
    vis_c = jnp.where(((n_c * CMP_STRIDE + (CMP_LEN - 1)) <= tq1) & (n_c < NC - 1), 0.0, NEG_INF)
    s_c = jnp.dot(kc_ref[...], q3_ref[...], preferred_element_type=f32) + tile3(vis_c)
    m_c = colmax(s_c)
    e_c = jnp.exp2(s_c - m_c)
    p_c = e_c * jnp.where(m_c > 0.5 * NEG_INF, 1.0 / colsum(e_c), 0.0)
    o_c = jnp.dot(vct_ref[...], p_c.astype(bf16), preferred_element_type=f32)
    p_sum = p_c[:, 0:QT]
    for r in range(1, R):
        p_sum = p_sum + p_c[:, r * QT:(r + 1) * QT]
    imp = _split_dot(p_sum, ovt_ref[...], 3, True)

    j = lax.broadcasted_iota(jnp.int32, (NB, 1), 0)
    cur = tq1 >> (SEL_LEN.bit_length() - 1)
    valid = (j * SEL_LEN) <= tq1
    forced = (j == 0) | (j == cur) | (j == cur - 1)
    score_ref[...] = jnp.where(valid, imp + FORCED_BONUS * forced.astype(f32), -jnp.inf)
    cnt_ref[...] = jnp.zeros_like(cnt_ref)
    nrb = NB // SUBLANES
    jsub = lax.broadcasted_iota(jnp.int32, (SUBLANES, 1), 0)
    last_rb = (t0 + QT - 1) // (SUBLANES * SEL_LEN)
    for ib in range(nrb):
        @pl.when(ib <= last_rb)
        def _(ib=ib):
            rows = [jnp.broadcast_to(score_ref[ib * SUBLANES + ii:ib * SUBLANES + ii + 1, :], (SUBLANES, QT))
                    for ii in range(SUBLANES)]
            for jb in range(nrb):
                @pl.when(jb <= last_rb)
                def _(jb=jb):
                    sj = score_ref[jb * SUBLANES:(jb + 1) * SUBLANES, :]
                    acc = cnt_ref[jb * SUBLANES:(jb + 1) * SUBLANES, :]
                    for ii in range(SUBLANES):
                        if ib < jb:
                            ahead = rows[ii] >= sj
                        elif ib > jb:
                            ahead = rows[ii] > sj
                        else:
                            ahead = (rows[ii] > sj) | ((rows[ii] == sj) & (ii < jsub))
                        acc = acc + jnp.where(ahead, 1.0, 0.0)
                    cnt_ref[jb * SUBLANES:(jb + 1) * SUBLANES, :] = acc
    selected = (cnt_ref[...] < float(min(SEL_TOPK, NB))) & valid
    sel_add = tile3(jnp.where(selected, 0.0, NEG_INF))
    spb = KB // SEL_LEN
    pad_rows = jnp.zeros((NSA_AUG - 4 - spb, W), f32)
    for tb in range(T // KB):
        selg_ref[tb] = jnp.concatenate([slope_rows, sel_add[tb * spb:(tb + 1) * spb], pad_rows], axis=0).astype(bf16)

    def stream(k_ref, vt_ref, n_plain, n_masked, start_of, mask_of, before_scores, s_buf, p_buf, acc_ref):
        n_all = n_plain + n_masked

        def scores(i, slot):
            ic = jnp.minimum(i, n_all - 1)
            before_scores(ic)
            s_buf[slot] = jnp.dot(k_ref[pl.ds(start_of(ic), KB), :], q3_ref[...], preferred_element_type=f32)

        def body(i, carry, masked, slot):
            m, l, alpha_prev = carry
            vblk = vt_ref[:, pl.ds(start_of(jnp.maximum(i - 1, 0)), KB)]
            pv = jnp.dot(vblk, p_buf[1 - slot], preferred_element_type=f32)
            scores(i + 1, 1 - slot)
            s = s_buf[slot]
            if masked:
                s = s + tile3(mask_of(i))
            m_new = jnp.maximum(m, colmax(s))
            alpha = jnp.exp2(m - m_new)
            p = jnp.exp2(s - m_new)
            p_buf[slot] = p.astype(bf16)
            acc_ref[...] = alpha_prev * acc_ref[...] + pv
            return m_new, alpha * l + colsum(p), alpha

        p_buf[1] = jnp.zeros((KB, W), bf16)
        acc_ref[...] = jnp.zeros_like(acc_ref)
        scores(0, 0)
        carry = (jnp.full((1, W), NEG_INF, f32), jnp.zeros((1, W), f32), jnp.ones((1, W), f32))

        def pair(j, carry, masked):
            return body(2 * j + 1, body(2 * j, carry, masked, 0), masked, 1)

        carry = lax.fori_loop(0, n_plain // 2, functools.partial(pair, masked=False), carry)
        _, l, alpha_last = lax.fori_loop(n_plain // 2, n_all // 2, functools.partial(pair, masked=True), carry)
        pv = jnp.dot(vt_ref[:, pl.ds(start_of(n_all - 1), KB)], p_buf[1], preferred_element_type=f32)
        return (alpha_last * acc_ref[...] + pv) * (1.0 / l)

    rel = lax.broadcasted_iota(jnp.int32, (1, QT), 1) - lax.broadcasted_iota(jnp.int32, (KB, 1), 0)

    nfull = qi * (QT // KB)

    def load_sel_rows(i):
        q3_ref[Dh:Dh + NSA_AUG, :] = selg_ref[i]

    o_s = stream(ks_ref, vst_ref, nfull, QT // KB,
                 lambda i: pl.multiple_of(i * KB, KB),
                 lambda i: jnp.where(rel - (i - nfull) * KB >= 0, 0.0, NEG_INF),
                 load_sel_rows, s_sel, p_sel, acc_sel)

    def win_mask(i):
        off = i * KB - WINDOW
        d = rel - off
        return jnp.where((d >= 0) & (d < WINDOW) & (t0 + off >= 0), 0.0, NEG_INF)

    o_w = stream(kw_ref, vwt_ref, 0, (WINDOW + QT) // KB,
                 lambda i: pl.multiple_of(jnp.maximum(t0 - WINDOW + i * KB, 0), KB),
                 win_mask, lambda i: None, s_win, p_win, acc_win)


    gates = _sigmoid(gl_ref[...])
    for r in range(R):
        ls = slice(r * QT, (r + 1) * QT)
        o_ref[r * Dh:(r + 1) * Dh, :] = (gates[3 * r:3 * r + 1, :] * o_c[:, ls]
                                          + gates[3 * r + 1:3 * r + 2, :] * o_s[:, ls]
                                          + gates[3 * r + 2:3 * r + 3, :] * o_w[:, ls])


def _alibi_slopes(n):
    def pow2(m):
        start = 2.0 ** (-8.0 / m)
        return [start ** (i + 1) for i in range(m)]
    if math.log2(n).is_integer():
        return pow2(n)
    c = 2 ** math.floor(math.log2(n))
    return pow2(c) + pow2(2 * c)[0::2][: n - c]


def _nsa_attend(q_t, k_cmp, v_cmp_t, ks, vs_t, kw, vw_t, gl_t):
    B, G, T, _ = ks.shape
    Dh = HEAD_DIM
    QT = NSA_QT
    R = NSA_Q_PER_KV
    NB = T // SEL_LEN
    NC = k_cmp.shape[2]
    assert QT % NSA_KB == 0 and WINDOW % NSA_KB == 0 and T % QT == 0 and NSA_KB % SEL_LEN == 0
    n = np.arange(NC)
    jb = np.arange(NB)
    ov = ((n[:, None] * CMP_STRIDE < jb[None, :] * SEL_LEN + SEL_LEN)
          & (n[:, None] * CMP_STRIDE + CMP_LEN - 1 >= jb[None, :] * SEL_LEN) & (n[:, None] < NC - 1))
    ovt = jnp.asarray(ov.T.astype(np.float32))
    per_bg = lambda shp: pl.BlockSpec((None, None) + shp, lambda b, g, i: (b, g, 0, 0))
    const = lambda shp: pl.BlockSpec(shp, lambda b, g, i: (0, 0))
    return pl.pallas_call(
        _nsa_body,
        grid=(B, G, T // QT),
        in_specs=[pl.BlockSpec((None, None, R * LANES, QT), lambda b, g, i: (b, g, 0, i)),
                  per_bg((NC, LANES)), per_bg((Dh, NC)),
                  per_bg((T, LANES)), per_bg((Dh, T)), per_bg((T, LANES)), per_bg((Dh, T)),
                  pl.BlockSpec((None, None, R * NSA_N_BRANCH, QT), lambda b, g, i: (b, g, 0, i)),
                  const((NB, NC))],
        out_specs=pl.BlockSpec((None, R * Dh, QT), lambda b, g, i: (b, g, i)),
        out_shape=jax.ShapeDtypeStruct((B, G * R * Dh, T), f32),
        scratch_shapes=[pltpu.VMEM((NB, QT), f32), pltpu.VMEM((NB, QT), f32), pltpu.VMEM((LANES, R * QT), bf16),
                        pltpu.VMEM((T // NSA_KB, NSA_AUG, R * QT), bf16),
                        pltpu.VMEM((2, NSA_KB, R * QT), f32), pltpu.VMEM((2, NSA_KB, R * QT), bf16),
                        pltpu.VMEM((Dh, R * QT), f32),
                        pltpu.VMEM((2, NSA_KB, R * QT), f32), pltpu.VMEM((2, NSA_KB, R * QT), bf16),
                        pltpu.VMEM((Dh, R * QT), f32)],
        compiler_params=_cparams("parallel", "parallel", "arbitrary"),
        name="nsa_attend",
    )(q_t, k_cmp, v_cmp_t, ks, vs_t, kw, vw_t, gl_t, ovt)


def _pos_cols(pos, sel_onehot):
    lo, hi = pos % LANES, pos // LANES
    cols = jnp.stack([lo, hi, lo, hi], axis=-1).astype(bf16)
    spb = NSA_KB // SEL_LEN
    blk = (pos // SEL_LEN) % spb
    hot = (blk[:, None] == jnp.arange(spb)[None, :]) & sel_onehot
    return jnp.concatenate([cols, hot.astype(bf16),
                            jnp.zeros((pos.shape[0], LANES - HEAD_DIM - 4 - spb), bf16)], axis=-1)


def _nsa(z_nsa, pe_k, pe_v, w1_k, w2_k, w1_v, w2_v):
    B, T, _ = z_nsa.shape
    G, Dh, R = NSA_KV_GROUPS, HEAD_DIM, NSA_Q_PER_KV
    log2e = math.log2(math.e)
    scale = HEAD_DIM ** -0.5 * log2e
    q = jnp.transpose((z_nsa[:, :, 0:NSA_W] * scale).reshape(B, T, G, R, Dh), (0, 2, 3, 4, 1))
    sl = np.asarray(_alibi_slopes(NSA_Q_HEADS), np.float64) * log2e
    sl_hi = sl.astype(bf16).astype(np.float64)
    sl_lo = sl - sl_hi
    rows = np.stack([sl_hi, sl_hi * LANES, sl_lo, sl_lo * LANES], axis=-1).astype(np.float32)
    srow = jnp.broadcast_to(jnp.asarray(rows.reshape(1, G, R, 4, 1)), (B, G, R, 4, T))
    q_t = jnp.concatenate([q, srow, jnp.zeros((B, G, R, LANES - Dh - 4, T), f32)], axis=3)
    q_t = q_t.reshape(B, G, R * LANES, T).astype(bf16)

    def kv(i):
        o = NSA_W + i * NSA_KV_W
        return z_nsa[:, :, o:o + NSA_KV_W].reshape(B, T, G, Dh).astype(bf16)
    nat = lambda t: jnp.transpose(t, (0, 2, 1, 3))
    trn = lambda t: jnp.transpose(t, (0, 2, 3, 1))
    rows16 = lambda t: nat(t).reshape(B, G, T // CMP_STRIDE, CMP_STRIDE * Dh)
    with_pos = lambda k, pos, hot=False: jnp.concatenate(
        [k, jnp.broadcast_to(_pos_cols(pos, hot), k.shape[:2] + (pos.shape[0], LANES - Dh))], axis=-1)
    k_cmp, v_cmp = _compress(rows16(kv(0)), rows16(kv(1)), pe_k, pe_v, w1_k, w2_k, w1_v, w2_v)
    tok = jnp.arange(T, dtype=jnp.int32)
    cmp_end = jnp.arange(T // CMP_STRIDE, dtype=jnp.int32) * CMP_STRIDE + (CMP_LEN - 1)
    gl_t = jnp.transpose(z_nsa[:, :, NSA_GATE_OFF:NSA_GATE_OFF + NSA_Q_HEADS * NSA_N_BRANCH]
                         .reshape(B, T, G, NSA_Q_PER_KV * NSA_N_BRANCH), (0, 2, 3, 1))
    o_t = _nsa_attend(q_t, with_pos(k_cmp.astype(bf16), cmp_end), jnp.swapaxes(v_cmp, 2, 3).astype(bf16),
                      with_pos(nat(kv(2)), tok, True), trn(kv(3)), with_pos(nat(kv(4)), tok), trn(kv(5)), gl_t)
    return jnp.swapaxes(o_t, 1, 2)


def _s5_body(u_ref, tg_ref, bend_ref, cpow_ref, are_ref, aim_ref, y_ref, start_ref, inc_ref):
    nk = u_ref.shape[0] // SUBLANES
    u = u_ref[...]
    inc_ref[...] = jnp.dot(u, bend_ref[...], preferred_element_type=f32)
    are = are_ref[...]
    aim = aim_ref[...]

    def step(kc, st):
        r0 = pl.multiple_of(kc * SUBLANES, SUBLANES)
        start_ref[pl.ds(r0, SUBLANES), :] = st
        return are * st + aim * pltpu.roll(st, S5_STATE, 1) + inc_ref[pl.ds(r0, SUBLANES), :]

    lax.fori_loop(0, nk, step, jnp.zeros((SUBLANES, 2 * S5_STATE), f32))
    y_ref[...] = (jnp.dot(u, tg_ref[...], preferred_element_type=f32)
                  + jnp.dot(start_ref[...].astype(bf16), cpow_ref[...], preferred_element_type=f32))


def _s5_tables(lam_re, lam_im, log_dt, b_re, b_im, c_re, c_im):
    Cs = S5_CHUNK
    lam_re, lam_im = lam_re.astype(f32), lam_im.astype(f32)
    b_re, b_im, c_re, c_im = (t.astype(f32) for t in (b_re, b_im, c_re, c_im))
    dt = jnp.exp(log_dt.astype(f32))[:, None]
    mag = jnp.exp(lam_re * dt)
    ab_re, ab_im = mag * jnp.cos(lam_im * dt), mag * jnp.sin(lam_im * dt)
    den = lam_re * lam_re + lam_im * lam_im
    f_re = ((ab_re - 1.0) * lam_re + ab_im * lam_im) / den
    f_im = (ab_im * lam_re - (ab_re - 1.0) * lam_im) / den
    bb_re = f_re[..., None] * b_re - f_im[..., None] * b_im
    bb_im = f_re[..., None] * b_im + f_im[..., None] * b_re
    tau = jnp.arange(Cs + 1, dtype=f32)[None, None, :]
    pmag = jnp.exp(lam_re[..., None] * dt[..., None] * tau)
    pang = lam_im[..., None] * dt[..., None] * tau
    pw_re, pw_im = pmag * jnp.cos(pang), pmag * jnp.sin(pang)
    ein = functools.partial(jnp.einsum, precision=HI)
    ca_re = c_re[..., None] * pw_re[:, None] - c_im[..., None] * pw_im[:, None]
    ca_im = c_re[..., None] * pw_im[:, None] + c_im[..., None] * pw_re[:, None]
    kern = ein('gopt,gpi->gtoi', ca_re[..., :Cs], bb_re) - ein('gopt,gpi->gtoi', ca_im[..., :Cs], bb_im)
    s_idx = jnp.arange(Cs)
    lag_np = np.arange(Cs)[None, :] - np.arange(Cs)[:, None]
    place = jnp.asarray((lag_np[:, :, None] == np.arange(Cs)[None, None, :]).astype(np.float32))
    tg = ein('stl,gloi->gsito', place, kern)
    tg = tg.reshape(S5_GROUPS, Cs * S5_CH, Cs * S5_CH)
    rev = (Cs - 1) - s_idx
    pe_re, pe_im = pw_re[:, :, rev], pw_im[:, :, rev]
    be_re = pe_re[..., None] * bb_re[:, :, None, :] - pe_im[..., None] * bb_im[:, :, None, :]
    be_im = pe_re[..., None] * bb_im[:, :, None, :] + pe_im[..., None] * bb_re[:, :, None, :]
    bend = jnp.concatenate([jnp.transpose(be_re, (0, 2, 3, 1)), jnp.transpose(be_im, (0, 2, 3, 1))], axis=-1)
    bend = bend.reshape(S5_GROUPS, Cs * S5_CH, 2 * S5_STATE)
    cp_re = jnp.transpose(ca_re[..., 1:], (0, 2, 3, 1))
    cp_im = jnp.transpose(ca_im[..., 1:], (0, 2, 3, 1))
    cpow = jnp.concatenate([cp_re, -cp_im], axis=1).reshape(S5_GROUPS, 2 * S5_STATE, Cs * S5_CH)
    a_re = jnp.concatenate([pw_re[..., Cs], pw_re[..., Cs]], axis=-1)[:, None, :]
    a_im = jnp.concatenate([-pw_im[..., Cs], pw_im[..., Cs]], axis=-1)[:, None, :]
    return tg.astype(bf16), bend.astype(bf16), cpow.astype(bf16), a_re, a_im


def _s5_scan(z_s5, tables):
    B, T, _ = z_s5.shape
    assert B == SUBLANES
    Cs = S5_CHUNK
    nk = T // Cs
    tg, bend, cpow, a_re, a_im = tables
    u = z_s5.reshape(B, nk, Cs, S5_GROUPS, S5_CH)
    u = jnp.transpose(u, (3, 1, 0, 2, 4)).reshape(S5_GROUPS, nk * B, Cs * S5_CH).astype(bf16)
    W = Cs * S5_CH
    per_g = lambda shp: pl.BlockSpec((None,) + shp, lambda g: (g, 0, 0))
    y = pl.pallas_call(
        _s5_body,
        grid=(S5_GROUPS,),
        in_specs=[per_g((nk * B, W)), per_g((W, W)), per_g((W, 2 * S5_STATE)), per_g((2 * S5_STATE, W)),
                  per_g((1, 2 * S5_STATE)), per_g((1, 2 * S5_STATE))],
        out_specs=per_g((nk * B, W)),
        out_shape=jax.ShapeDtypeStruct((S5_GROUPS, nk * B, W), f32),
        scratch_shapes=[pltpu.VMEM((nk * B, 2 * S5_STATE), f32), pltpu.VMEM((nk * B, 2 * S5_STATE), f32)],
        compiler_params=_cparams("parallel"),
        name="s5_conv",
    )(u, tg, bend, cpow, a_re, a_im)
    y = y.reshape(S5_GROUPS, nk, B, Cs, S5_CH)
    return jnp.transpose(y, (2, 1, 3, 0, 4)).reshape(B, T, S5_W)


def _s5_post_body(y_ref, u_ref, d_ref, w_ref, o_ref):
    y = _gelu_tanh(y_ref[...] + d_ref[...] * u_ref[...])
    vg = _dot(y, w_ref[...])
    o_ref[...] = vg[:, 0:S5_W] * _sigmoid(vg[:, S5_W:2 * S5_W])


def _s5_post(y2d, u2d, d_skip, w_glu):
    n = y2d.shape[0]
    tm = min(S5P_TM, n)
    blk = pl.BlockSpec((tm, S5_W), lambda i: (i, 0))
    return pl.pallas_call(
        _s5_post_body,
        grid=(n // tm,),
        in_specs=[blk, blk, pl.BlockSpec((1, S5_W), lambda i: (0, 0)),
                  pl.BlockSpec((S5_W, 2 * S5_W), lambda i: (0, 0))],
        out_specs=blk,
        out_shape=jax.ShapeDtypeStruct((n, S5_W), f32),
        compiler_params=_cparams("parallel"),
        name="s5_glu",
    )(y2d, u2d, d_skip.reshape(1, S5_W), w_glu.astype(bf16))


def _out_proj_body(h_ref, orw_ref, onsa_ref, os5_ref, w_ref, g_ref, o_ref):
    mix = (_dot(orw_ref[...], w_ref[0:RWKV_W, :])
           + _dot(onsa_ref[...], w_ref[RWKV_W:RWKV_W + NSA_W, :])
           + _dot(os5_ref[...], w_ref[RWKV_W + NSA_W:D_MIX, :]))
    o_ref[...] = h_ref[...] + _rms(mix, g_ref[...])


def _out_proj(h2d, o_rw, o_nsa, o_s5, w_out, g):
    n = h2d.shape[0]
    tm = min(OUT_TM, n)
    rows = lambda w: pl.BlockSpec((tm, w), lambda i: (i, 0))
    return pl.pallas_call(
        _out_proj_body,
        grid=(n // tm,),
        in_specs=[rows(D_MODEL), rows(RWKV_W), rows(NSA_W), rows(S5_W),
                  pl.BlockSpec((D_MIX, D_MODEL), lambda i: (0, 0)),
                  pl.BlockSpec((1, D_MODEL), lambda i: (0, 0))],
        out_specs=rows(D_MODEL),
        out_shape=jax.ShapeDtypeStruct((n, D_MODEL), f32),
        compiler_params=_cparams("parallel"),
        name="out_proj",
    )(h2d, o_rw, o_nsa, o_s5, w_out.astype(bf16), g.reshape(1, D_MODEL))


def _ffn_body(h_ref, hp_ref, gpre_ref, wg_ref, wu_ref, cwg_ref, cwu_ref, cbg_ref, cbu_ref, wd_ref, gpost_ref,
              o_ref, xn_ref, acc_ref, *, tiles_per_seq):
    tm = h_ref.shape[0]
    H = SUBLANES
    c = pl.program_id(1)

    @pl.when(c == 0)
    def _():
        first = (pl.program_id(0) % tiles_per_seq) == 0
        xp = _rms(hp_ref[...], gpre_ref[...])
        xn_ref[0:H, :] = jnp.where(first, 0.0, xp).astype(bf16)
        xn_ref[H:H + tm, :] = _rms(h_ref[...], gpre_ref[...]).astype(bf16)
        acc_ref[...] = jnp.zeros_like(acc_ref)

    xn = xn_ref[...]

    def conv_branch(w_ref, cw_ref, cb_ref):
        hu = jnp.dot(xn, w_ref[...], preferred_element_type=f32)
        cw = cw_ref[...]
        out = (cw[0:1, :] * pltpu.roll(hu, 2, 0)[H:H + tm]
               + cw[1:2, :] * pltpu.roll(hu, 1, 0)[H:H + tm]
               + cw[2:3, :] * hu[H:H + tm])
        return out + cb_ref[...]

    gate = conv_branch(wg_ref, cwg_ref, cbg_ref)
    up = conv_branch(wu_ref, cwu_ref, cbu_ref)
    acc_ref[...] += _dot(_gelu_tanh(gate) * up, wd_ref[...])

    @pl.when(c == pl.num_programs(1) - 1)
    def _():
        o_ref[...] = h_ref[...] + _rms(acc_ref[...], gpost_ref[...])


def _conv_ffn(h2d, seq_len, g_pre, w_up, conv_w, conv_b, w_down, g_post):
    n = h2d.shape[0]
    tm = min(FFN_TM, seq_len)
    tc = FFN_TC
    nc = D_FF // tc
    H = SUBLANES
    tiles_per_seq = seq_len // tm
    hpb = tm // H
    w_up = w_up.astype(bf16)
    cb = conv_b.reshape(1, 2 * D_FF)
    return pl.pallas_call(
        functools.partial(_ffn_body, tiles_per_seq=tiles_per_seq),
        grid=(n // tm, nc),
        in_specs=[pl.BlockSpec((tm, D_MODEL), lambda i, c: (i, 0)),
                  pl.BlockSpec((H, D_MODEL), lambda i, c: (jnp.maximum(i * hpb - 1, 0), 0)),
                  pl.BlockSpec((1, D_MODEL), lambda i, c: (0, 0)),
                  pl.BlockSpec((D_MODEL, tc), lambda i, c: (0, c)),
                  pl.BlockSpec((D_MODEL, tc), lambda i, c: (0, nc + c)),
                  pl.BlockSpec((CONV_W, tc), lambda i, c: (0, c)),
                  pl.BlockSpec((CONV_W, tc), lambda i, c: (0, nc + c)),
                  pl.BlockSpec((1, tc), lambda i, c: (0, c)),
                  pl.BlockSpec((1, tc), lambda i, c: (0, nc + c)),
                  pl.BlockSpec((tc, D_MODEL), lambda i, c: (c, 0)),
                  pl.BlockSpec((1, D_MODEL), lambda i, c: (0, 0))],
        out_specs=pl.BlockSpec((tm, D_MODEL), lambda i, c: (i, 0)),
        out_shape=jax.ShapeDtypeStruct((n, D_MODEL), f32),
        scratch_shapes=[pltpu.VMEM((H + tm, D_MODEL), bf16), pltpu.VMEM((tm, D_MODEL), f32)],
        compiler_params=_cparams("parallel", "arbitrary"),
        name="conv_ffn",
    )(h2d, h2d, g_pre.reshape(1, D_MODEL), w_up, w_up, conv_w, conv_w, cb, cb, w_down.astype(bf16),
      g_post.reshape(1, D_MODEL))


def _ple_body(h_ref, p_ref, wp_ref, gp_ref, wg_ref, o_ref):
    h = h_ref[...]
    e = _rms(_dot(p_ref[...], wp_ref[...]), gp_ref[...])
    gate = _sigmoid(_dot(h, wg_ref[...]))
    o_ref[...] = h + gate * e


def _ple(h2d, p2d, w_ple, g_ple, w_gate):
    n = h2d.shape[0]
    tm = min(PLE_TM, n)
    return pl.pallas_call(
        _ple_body,
        grid=(n // tm,),
        in_specs=[pl.BlockSpec((tm, D_MODEL), lambda i: (i, 0)),
                  pl.BlockSpec((tm, PLE_DIM), lambda i: (i, 0)),
                  pl.BlockSpec((PLE_DIM, D_MODEL), lambda i: (0, 0)),
                  pl.BlockSpec((1, D_MODEL), lambda i: (0, 0)),
                  pl.BlockSpec((D_MODEL, D_MODEL), lambda i: (0, 0))],
        out_specs=pl.BlockSpec((tm, D_MODEL), lambda i: (i, 0)),
        out_shape=jax.ShapeDtypeStruct((n, D_MODEL), f32),
        compiler_params=_cparams("parallel"),
        name="ple_gate",
    )(h2d, p2d, w_ple.astype(bf16), g_ple.reshape(1, D_MODEL), w_gate.astype(bf16))


def kernel(x, p, pre_mix_norm, post_mix_norm, pre_ffn_norm, post_ffn_norm, w_in, w_out, shift_mu, rw_w0, rw_w2, rw_a0, rw_a2, rw_g2, rw_k_k, rw_k_a, rw_r_k, rw_gn_w, rw_gn_b, cmp_pe_k, cmp_pe_v, cmp_w1_k, cmp_w2_k, cmp_w1_v, cmp_w2_v, s5_lam_re, s5_lam_im, s5_log_dt, s5_b_re, s5_b_im, s5_c_re, s5_c_im, s5_d, s5_w_glu, w_up, conv_w, conv_b, w_down, w_ple, ple_norm, w_ple_gate):
    B, T, D = x.shape
    n = B * T
    depth = w_in.shape[0]
    h = x.reshape(n, D)
    for i in range(depth):
        w_p, mu_p = _pad_in_weights(w_in[i], shift_mu[i])
        z_rw, z_nsa, z_s5 = _in_proj(h, pre_mix_norm[i], w_p)
        o_rw = _rwkv(z_rw.reshape(B, T, RW_P), mu_p, rw_w0[i], rw_w2[i], rw_a0[i], rw_a2[i], rw_g2[i],
                     rw_k_k[i], rw_k_a[i], rw_r_k[i], rw_gn_w[i], rw_gn_b[i])
        o_nsa = _nsa(z_nsa.reshape(B, T, NSA_P), cmp_pe_k[i], cmp_pe_v[i], cmp_w1_k[i], cmp_w2_k[i],
                     cmp_w1_v[i], cmp_w2_v[i])
        tables = _s5_tables(s5_lam_re[i], s5_lam_im[i], s5_log_dt[i], s5_b_re[i], s5_b_im[i], s5_c_re[i], s5_c_im[i])
        y_s5 = _s5_scan(z_s5.reshape(B, T, S5_W), tables)
        o_s5 = _s5_post(y_s5.reshape(n, S5_W), z_s5, s5_d[i], s5_w_glu[i])
        h = _out_proj(h, o_rw.reshape(n, RWKV_W), o_nsa.reshape(n, NSA_W), o_s5, w_out[i], post_mix_norm[i])
        h = _conv_ffn(h, T, pre_ffn_norm[i], w_up[i], conv_w[i], conv_b[i], w_down[i], post_ffn_norm[i])
        h = _ple(h, p[i].reshape(n, PLE_DIM), w_ple[i], ple_norm[i], w_ple_gate[i])
    return h.reshape(B, T, D)
```

```python
import functools
import math

import numpy as np
import jax
import jax.numpy as jnp
from jax import lax
from jax.experimental import pallas as pl
from jax.experimental.pallas import tpu as pltpu

f32 = jnp.float32
bf16 = jnp.bfloat16
HI = lax.Precision.HIGHEST

D_MODEL = 1024
HEAD_DIM = 64
RWKV_HEADS = 6
RWKV_W = RWKV_HEADS * HEAD_DIM
D_DECAY_LORA = 64
D_AAA_LORA = 64
D_GATE_LORA = 128
RWKV_COLS = 3 * RWKV_W + D_DECAY_LORA + D_AAA_LORA + D_GATE_LORA
GN_EPS = 64e-5
NSA_Q_HEADS = 6
NSA_KV_GROUPS = 2
NSA_Q_PER_KV = NSA_Q_HEADS // NSA_KV_GROUPS
NSA_W = NSA_Q_HEADS * HEAD_DIM
NSA_KV_W = NSA_KV_GROUPS * HEAD_DIM
NSA_N_BRANCH = 3
NSA_COLS = NSA_W + 6 * NSA_KV_W + NSA_Q_HEADS * NSA_N_BRANCH
CMP_LEN = 32
CMP_STRIDE = 16
CMP_HIDDEN = 128
SEL_LEN = 64
SEL_TOPK = 16
WINDOW = 512
FORCED_BONUS = 1e3
NEG_INF = -1e30
S5_GROUPS = 16
S5_CH = 16
S5_W = S5_GROUPS * S5_CH
S5_STATE = 64
D_MIX = RWKV_W + NSA_W + S5_W
D_FF = 2816
CONV_W = 3
PLE_DIM = 256
NORM_EPS = 1e-6

LANES = 128
SUBLANES = 8
VMEM_LIMIT = 56 * 1024 * 1024

RW_LORA_W_OFF = 3 * RWKV_W
RW_LORA_A_OFF = RW_LORA_W_OFF + LANES
RW_LORA_G_OFF = RW_LORA_A_OFF + LANES
RW_P = RW_LORA_G_OFF + D_GATE_LORA
NSA_GATE_OFF = NSA_W + 6 * NSA_KV_W
NAT_S5 = RW_P
NAT_KSW = NAT_S5 + S5_W
NAT_KCV = NAT_KSW + NSA_KV_GROUPS * 2 * LANES
NAT_P = NAT_KCV + NSA_KV_GROUPS * LANES
GL_ROWS = 16
TR_VS = NSA_Q_HEADS * LANES
TR_VW = TR_VS + NSA_KV_W
TR_GL = TR_VW + NSA_KV_W
TR_P = TR_GL + NSA_KV_GROUPS * GL_ROWS

IN_TM = 256
RW_CHUNK = 64
RW_NCH = 4
NSA_QT = 512
NSA_KB = 256
NSA_AUG = 16
S5_CHUNK = 64
OUT_TM = 512
FFN_TM = 1024
FFN_TC = 256
PLE_TM = 512
S5P_TM = 1024


def _cparams(*sem):
    return pltpu.CompilerParams(dimension_semantics=sem, vmem_limit_bytes=VMEM_LIMIT)


def _rms(x, g):
    ms = jnp.mean(x * x, axis=-1, keepdims=True)
    return x * lax.rsqrt(ms + NORM_EPS) * g


def _gelu_tanh(x):
    return 0.5 * x * (1.0 + jnp.tanh(math.sqrt(2.0 / math.pi) * (x + 0.044715 * (x * x * x))))


def _sigmoid(x):
    return 1.0 / (1.0 + jnp.exp(-x))


def _softplus(x):
    return jnp.maximum(x, 0.0) + jnp.log(1.0 + jnp.exp(-jnp.abs(x)))


def _dot(a, b):
    return jnp.dot(a.astype(bf16), b.astype(bf16), preferred_element_type=f32)


def _dot_nt(a, b):
    return lax.dot_general(a.astype(bf16), b.astype(bf16), (((1,), (1,)), ((), ())), preferred_element_type=f32)


def _dot_tn(a, b):
    return lax.dot_general(a.astype(bf16), b.astype(bf16), (((0,), (0,)), ((), ())), preferred_element_type=f32)


def _dot_hi(a, b):
    return jnp.dot(a, b, preferred_element_type=f32, precision=HI)


def _split_dot(x, m01, terms, m_left):
    m = m01.astype(bf16)
    acc = None
    rem = x
    for _ in range(terms):
        piece = rem.astype(bf16)
        d = (jnp.dot(m, piece, preferred_element_type=f32) if m_left
             else jnp.dot(piece, m, preferred_element_type=f32))
        acc = d if acc is None else acc + d
        rem = rem - piece.astype(f32)
    return acc


def _key_tail(t, sel_onehot):
    col = lax.broadcasted_iota(jnp.int32, t.shape[:1] + (LANES,), 1) - HEAD_DIM
    lo = (t & (LANES - 1)).astype(f32)
    hi = (t >> (LANES.bit_length() - 1)).astype(f32)
    tail = jnp.where((col == 0) | (col == 2), lo, jnp.where((col == 1) | (col == 3), hi, 0.0))
    if sel_onehot:
        spb = NSA_KB // SEL_LEN
        blk = (t >> (SEL_LEN.bit_length() - 1)) & (spb - 1)
        tail = tail + jnp.where(col - 4 == blk, 1.0, 0.0) * jnp.where((col >= 4) & (col < 4 + spb), 1.0, 0.0)
    return tail


def _in_proj_body(x_ref, g_ref, wn_ref, wt_ref, qaug_ref,
                  zrw_ref, zs5_ref, ks_ref, kw_ref, kcv_ref, qt_ref, vst_ref, vwt_ref, glt_ref):
    tm = x_ref.shape[0]
    G, Dh = NSA_KV_GROUPS, HEAD_DIM
    xn = _rms(x_ref[...], g_ref[...]).astype(bf16)
    zrw_ref[...] = jnp.dot(xn, wn_ref[:, 0:RW_P], preferred_element_type=f32)
    zs5_ref[...] = jnp.dot(xn, wn_ref[:, NAT_S5:NAT_KSW], preferred_element_type=f32)
    ksw = jnp.dot(xn, wn_ref[:, NAT_KSW:NAT_KCV], preferred_element_type=f32)
    tok = pl.program_id(1) * tm + lax.broadcasted_iota(jnp.int32, (tm, 1), 0)
    tail_w = _key_tail(tok, False)
    tail_s = _key_tail(tok, True)
    for g in range(G):
        ks_ref[g] = (ksw[:, 2 * g * LANES:(2 * g + 1) * LANES] + tail_s).astype(bf16)
        kw_ref[g] = (ksw[:, (2 * g + 1) * LANES:(2 * g + 2) * LANES] + tail_w).astype(bf16)
    kcv = jnp.dot(xn, wn_ref[:, NAT_KCV:NAT_P], preferred_element_type=f32)
    for g in range(G):
        kcv_ref[g] = kcv[:, g * LANES:(g + 1) * LANES].astype(bf16)
    zt = lax.dot_general(wt_ref[...], xn, (((1,), (1,)), ((), ())), preferred_element_type=f32)
    qscale = HEAD_DIM ** -0.5 * math.log2(math.e)
    qt_ref[...] = (zt[0:TR_VS] * qscale + qaug_ref[...]).astype(bf16)
    for g in range(G):
        vst_ref[g] = zt[TR_VS + g * Dh:TR_VS + (g + 1) * Dh].astype(bf16)
        vwt_ref[g] = zt[TR_VW + g * Dh:TR_VW + (g + 1) * Dh].astype(bf16)
    glt_ref[...] = zt[TR_GL:TR_P]


def _in_proj(h, g, wn, wt, qaug):
    B, T, _ = h.shape
    tm = min(IN_TM, T)
    G, Dh = NSA_KV_GROUPS, HEAD_DIM
    const = lambda shp: pl.BlockSpec(shp, lambda b, i: (0,) * len(shp))
    tok3 = lambda w: pl.BlockSpec((None, tm, w), lambda b, i: (b, i, 0))
    tok4 = lambda w: pl.BlockSpec((None, G, tm, w), lambda b, i: (b, 0, i, 0))
    trn3 = lambda r: pl.BlockSpec((None, r, tm), lambda b, i: (b, 0, i))
    trn4 = pl.BlockSpec((None, G, Dh, tm), lambda b, i: (b, 0, 0, i))
    S = jax.ShapeDtypeStruct
    return pl.pallas_call(
        _in_proj_body,
        grid=(B, T // tm),
        in_specs=[tok3(D_MODEL), const((1, D_MODEL)), const((D_MODEL, NAT_P)), const((TR_P, D_MODEL)),
                  const((TR_VS, 1))],
        out_specs=[tok3(RW_P), tok3(S5_W), tok4(LANES), tok4(LANES), tok4(LANES),
                   trn3(TR_VS), trn4, trn4, trn3(TR_P - TR_GL)],
        out_shape=[S((B, T, RW_P), f32), S((B, T, S5_W), f32), S((B, G, T, LANES), bf16), S((B, G, T, LANES), bf16),
                   S((B, G, T, LANES), bf16), S((B, TR_VS, T), bf16), S((B, G, Dh, T), bf16), S((B, G, Dh, T), bf16),
                   S((B, TR_P - TR_GL, T), f32)],
        compiler_params=_cparams("parallel", "parallel"),
        name="in_proj",
    )(h, g.reshape(1, D_MODEL), wn, wt, qaug)


def _prep_in_weights(w_in, shift_mu):
    d = w_in.shape[0]
    G, R, Dh = NSA_KV_GROUPS, NSA_Q_PER_KV, HEAD_DIM
    zc = lambda k: jnp.zeros((d, k), w_in.dtype)
    o = 3 * RWKV_W
    rw = [w_in[:, :o],
          w_in[:, o:o + D_DECAY_LORA], zc(LANES - D_DECAY_LORA),
          w_in[:, o + D_DECAY_LORA:o + D_DECAY_LORA + D_AAA_LORA], zc(LANES - D_AAA_LORA),
          w_in[:, o + D_DECAY_LORA + D_AAA_LORA:RWKV_COLS]]
    nsa = w_in[:, RWKV_COLS:RWKV_COLS + NSA_COLS]
    s5 = w_in[:, RWKV_COLS + NSA_COLS:]
    sec = lambda i, g: nsa[:, NSA_W + i * NSA_KV_W + g * Dh:NSA_W + i * NSA_KV_W + (g + 1) * Dh]
    ksw = [t for g in range(G) for t in (sec(2, g), zc(LANES - Dh), sec(4, g), zc(LANES - Dh))]
    kcv = [t for g in range(G) for t in (sec(0, g), sec(1, g))]
    wn = jnp.concatenate(rw + [s5] + ksw + kcv, axis=1).astype(bf16)
    q_cols = [t for h in range(G * R) for t in (nsa[:, h * Dh:(h + 1) * Dh], zc(LANES - Dh))]
    nb = R * NSA_N_BRANCH
    gl_cols = [t for g in range(G) for t in (nsa[:, NSA_GATE_OFF + g * nb:NSA_GATE_OFF + (g + 1) * nb],
                                             zc(GL_ROWS - nb))]
    wt = jnp.concatenate(q_cols + [sec(3, g) for g in range(G)] + [sec(5, g) for g in range(G)] + gl_cols, axis=1)
    wt = wt.T.astype(bf16)
    sl = np.asarray(_alibi_slopes(NSA_Q_HEADS), np.float64) * math.log2(math.e)
    sl_hi = sl.astype(bf16).astype(np.float64)
    sl_lo = sl - sl_hi
    qaug = np.zeros((G * R, LANES), np.float32)
    qaug[:, Dh:Dh + 4] = np.stack([sl_hi, sl_hi * LANES, sl_lo, sl_lo * LANES], axis=-1)
    z1 = lambda k: jnp.zeros((k,), shift_mu.dtype)
    mu_p = jnp.concatenate([shift_mu[:o],
                            shift_mu[o:o + D_DECAY_LORA], z1(LANES - D_DECAY_LORA),
                            shift_mu[o + D_DECAY_LORA:o + D_DECAY_LORA + D_AAA_LORA], z1(LANES - D_AAA_LORA),
                            shift_mu[o + D_DECAY_LORA + D_AAA_LORA:]])
    return wn, wt, jnp.asarray(qaug.reshape(TR_VS, 1)), mu_p.reshape(1, RW_P)


def _rwkv_body(z_ref, mu_ref, w0_ref, w2_ref, a0_ref, a2_ref, g2_ref, kk_ref, ka_ref, rk_ref, gnw_ref, gnb_ref,
               o_ref, state_ref, prev_ref):
    C = RW_CHUNK
    N = HEAD_DIM
    L = z_ref.shape[0]
    nch = L // C
    npair = RWKV_W // LANES

    @pl.when(pl.program_id(1) == 0)
    def _():
        state_ref[...] = jnp.zeros_like(state_ref)
        prev_ref[...] = jnp.zeros_like(prev_ref)

    z = z_ref[...]
    row = lax.broadcasted_iota(jnp.int32, z.shape, 0)
    zprev = jnp.where(row == 0, prev_ref[0:1, :], pltpu.roll(z, 1, 0))
    prev_ref[0:1, :] = z[L - 1:L, :]
    zs = z + (zprev - z) * mu_ref[...]

    r = zs[:, 0:RWKV_W]
    k = zs[:, RWKV_W:2 * RWKV_W]
    v = zs[:, 2 * RWKV_W:3 * RWKV_W]
    wl = zs[:, RW_LORA_W_OFF:RW_LORA_W_OFF + LANES]
    al = zs[:, RW_LORA_A_OFF:RW_LORA_A_OFF + LANES]
    gl = zs[:, RW_LORA_G_OFF:RW_LORA_G_OFF + D_GATE_LORA]
    lane = lax.broadcasted_iota(jnp.int32, (1, LANES), 1)
    li = lax.broadcasted_iota(jnp.int32, (LANES, LANES), 0)
    lj = lax.broadcasted_iota(jnp.int32, (LANES, LANES), 1)
    hshift = N.bit_length() - 1
    bd128 = ((li >> hshift) == (lj >> hshift)).astype(f32)
    eye128 = (li == lj).astype(f32)
    first_half = lane < N
    lane3 = lax.broadcasted_iota(jnp.int32, (1, RWKV_W), 1)
    half_mask = [(((lane3 >> hshift) & 1) == hh).astype(f32) for hh in range(2)]

    def head_sums(x):
        return jnp.concatenate([_split_dot(x[:, j * LANES:(j + 1) * LANES], bd128, 2, False)
                                for j in range(npair)], axis=1)

    w = -_softplus(-(w0_ref[...] + _dot(jnp.tanh(wl), w2_ref[...]))) - 0.5
    logd = -jnp.exp(w)
    a = _sigmoid(a0_ref[...] + _dot(al, a2_ref[...]))
    g = _dot(_sigmoid(gl), g2_ref[...])
    kkr = k * kk_ref[...]
    kk = kkr / jnp.maximum(jnp.sqrt(head_sums(kkr * kkr)), 1e-12)
    kmod = k * (1.0 + (a - 1.0) * ka_ref[...])
    b = kk * a

    ti = lax.broadcasted_iota(jnp.int32, (C, C), 0)
    si = lax.broadcasted_iota(jnp.int32, (C, C), 1)
    tri_incl = (si <= ti).astype(f32)
    tri_strict = (si < ti).astype(f32)
    eye = (si == ti).astype(f32)
    cums, tots = [], []
    for c in range(nch):
        cc = _split_dot(logd[c * C:(c + 1) * C], tri_incl, 3, True)
        cums.append(cc)
        tots.append(jnp.broadcast_to(cc[C - 1:C], (C, RWKV_W)))
    cum = jnp.concatenate(cums, axis=0)
    tot = jnp.concatenate(tots, axis=0)
    e_inv = jnp.exp(-cum)
    e_end = jnp.exp(tot - cum)
    r_t = r * jnp.exp(cum)
    kk_t = kk * jnp.exp(cum - logd)
    k_h = kmod * e_inv
    b_h = b * e_inv
    k_e = kmod * e_end
    b_e = b * e_end
    e_tot = jnp.exp(tot)
    kk_t_m = [kk_t * m for m in half_mask]
    r_t_m = [r_t * m for m in half_mask]
    k_e_m = [k_e * m for m in half_mask]
    b_e_m = [b_e * m for m in half_mask]

    lvl_masks = []
    s = 1
    while s < C:
        sh = s.bit_length() - 1
        m = ((ti >> (sh + 1)) == (si >> (sh + 1))) & (((ti >> sh) & 1) == 1) & (((si >> sh) & 1) == 0)
        lvl_masks.append(m.astype(f32))
        s *= 2

    chains = [(c, h) for c in range(nch) for h in range(RWKV_HEADS)]

    def blk(arr, c, h):
        j = h // 2
        return arr[c * C:(c + 1) * C, j * LANES:(j + 1) * LANES]

    p = [_dot_nt(jnp.concatenate([blk(kk_t_m[h % 2], c, h), blk(r_t_m[h % 2], c, h)], axis=0),
                 jnp.concatenate([blk(b_h, c, h), blk(k_h, c, h)], axis=0)) for c, h in chains]
    a_ab = [t[0:C, 0:C] * tri_strict for t in p]
    a_ak = [t[0:C, C:2 * C] * tri_strict for t in p]
    a_rb = [t[C:2 * C, 0:C] * tri_incl for t in p]
    a_rk = [t[C:2 * C, C:2 * C] * tri_incl for t in p]
    x = [eye - t * lvl_masks[0] for t in a_ab]
    for m in lvl_masks[1:]:
        t1 = [_dot(xg, ag * m) for xg, ag in zip(x, a_ab)]
        x = [xg - _dot(tg, xg) for xg, tg in zip(x, t1)]
    av = [_dot(jnp.concatenate([a_ak[i], a_rk[i]], axis=0), blk(v, c, h)) for i, (c, h) in enumerate(chains)]
    wu = [_dot(x[i], jnp.concatenate([blk(kk_t, c, h), av[i][0:C]], axis=1)) for i, (c, h) in enumerate(chains)]
    bwu = [_dot_tn(blk(b_e_m[h % 2], c, h), wu[i]) for i, (c, h) in enumerate(chains)]
    ktv = [_dot_tn(blk(k_e_m[h % 2], c, h), blk(v, c, h)) for c, h in chains]
    arb = [_dot(a_rb[i], wu[i]) for i in range(len(chains))]

    state = [state_ref[j] for j in range(npair)]
    out_rows = []
    for c in range(nch):
        out_lanes = []
        for j in range(npair):
            ge = c * RWKV_HEADS + 2 * j
            go = ge + 1
            bsum = bwu[ge] + bwu[go]
            tm = eye128 * e_tot[c * C:c * C + 1, j * LANES:(j + 1) * LANES] - bsum[:, 0:LANES] * bd128
            zc = (ktv[ge] + ktv[go] - bsum[:, LANES:2 * LANES]) * bd128
            q = blk(r_t, c, 2 * j) - jnp.where(first_half, arb[ge][:, 0:LANES], arb[go][:, 0:LANES])
            o_in = jnp.where(first_half, av[ge][C:2 * C] - arb[ge][:, LANES:2 * LANES],
                             av[go][C:2 * C] - arb[go][:, LANES:2 * LANES])
            ts = _dot(jnp.concatenate([tm, q], axis=0), state[j])
            state[j] = ts[0:LANES] + zc
            out_lanes.append(ts[LANES:LANES + C] + o_in)
        out_rows.append(jnp.concatenate(out_lanes, axis=1))
    for j in range(npair):
        state_ref[j] = state[j]
    o = jnp.concatenate(out_rows, axis=0)

    inv_n = 1.0 / N
    mean = head_sums(o) * inv_n
    oc = o - mean
    var = head_sums(oc * oc) * inv_n
    on = oc * lax.rsqrt(var + GN_EPS) * gnw_ref[...] + gnb_ref[...]
    bonus = head_sums(r * kmod * rk_ref[...]) * v
    o_ref[...] = (on + bonus) * g


def _rwkv(z_rw, mu_p, w0, w2, a0, a2, g2, k_k, k_a, r_k, gn_w, gn_b):
    B, T, _ = z_rw.shape
    L = RW_CHUNK * RW_NCH
    row = lambda t: t.reshape(1, RWKV_W)
    w2p = jnp.concatenate([w2, jnp.zeros((LANES - D_DECAY_LORA, RWKV_W), w2.dtype)], axis=0).astype(bf16)
    a2p = jnp.concatenate([a2, jnp.zeros((LANES - D_AAA_LORA, RWKV_W), a2.dtype)], axis=0).astype(bf16)
    const = lambda shp: pl.BlockSpec(shp, lambda b, c: (0,) * len(shp))
    return pl.pallas_call(
        _rwkv_body,
        grid=(B, T // L),
        in_specs=[pl.BlockSpec((None, L, RW_P), lambda b, c: (b, c, 0)),
                  const((1, RW_P)), const((1, RWKV_W)), const((LANES, RWKV_W)), const((1, RWKV_W)),
                  const((LANES, RWKV_W)), const((D_GATE_LORA, RWKV_W)), const((1, RWKV_W)), const((1, RWKV_W)),
                  const((1, RWKV_W)), const((1, RWKV_W)), const((1, RWKV_W))],
        out_specs=pl.BlockSpec((None, L, RWKV_W), lambda b, c: (b, c, 0)),
        out_shape=jax.ShapeDtypeStruct((B, T, RWKV_W), f32),
        scratch_shapes=[pltpu.VMEM((RWKV_W // LANES, LANES, LANES), f32),
                        pltpu.VMEM((SUBLANES, RW_P), f32)],
        compiler_params=_cparams("parallel", "arbitrary"),
        name="rwkv7",
    )(z_rw, mu_p, row(w0), w2p, row(a0), a2p, g2.astype(bf16), row(k_k), row(k_a), row(r_k), row(gn_w), row(gn_b))


def _cmp_body(x2_ref, w1k_ref, w1v_ref, c0k_ref, c0v_ref, w2k_ref, w2v_ref, ko_ref, vto_ref):
    half = w1k_ref.shape[0] // 2
    x2 = x2_ref[...]
    nrow = x2.shape[0]

    def mlp(w1_ref, c0_ref, w2_ref):
        ha = jnp.dot(x2, w1_ref[0:half, :], preferred_element_type=f32)
        hb = jnp.dot(x2, w1_ref[half:2 * half, :], preferred_element_type=f32)
        hb_next = pltpu.roll(hb, nrow - 1, 0)
        out = _dot(_gelu_tanh(ha + hb_next + c0_ref[...]), w2_ref[...])
        n = lax.broadcasted_iota(jnp.int32, out.shape, 0)
        return jnp.where(n < nrow - 1, out, 0.0)

    k_cmp = mlp(w1k_ref, c0k_ref, w2k_ref)
    n1 = lax.broadcasted_iota(jnp.int32, (nrow, 1), 0)
    tail = _key_tail(n1 * CMP_STRIDE + (CMP_LEN - 1), False)
    ko_ref[...] = (jnp.concatenate([k_cmp, jnp.zeros((nrow, LANES - HEAD_DIM), f32)], axis=1) + tail).astype(bf16)
    vto_ref[...] = mlp(w1v_ref, c0v_ref, w2v_ref).T.astype(bf16)


def _compress(kcv, pe_k, pe_v, w1_k, w2_k, w1_v, w2_v):
    B, G, T, _ = kcv.shape
    M = T // CMP_STRIDE
    Dh = HEAD_DIM
    x2 = kcv.reshape(B, G, M, CMP_STRIDE * LANES)
    w3k = w1_k.reshape(CMP_LEN, Dh, CMP_HIDDEN)
    w3v = w1_v.reshape(CMP_LEN, Dh, CMP_HIDDEN)
    zeros = jnp.zeros_like(w3k)
    w1k_e = jnp.concatenate([w3k, zeros], axis=1).reshape(CMP_LEN * LANES, CMP_HIDDEN).astype(bf16)
    w1v_e = jnp.concatenate([zeros, w3v], axis=1).reshape(CMP_LEN * LANES, CMP_HIDDEN).astype(bf16)
    flat = CMP_LEN * Dh
    c0k = jnp.dot(pe_k.reshape(1, flat), w1_k, precision=HI)
    c0v = jnp.dot(pe_v.reshape(1, flat), w1_v, precision=HI)
    const = lambda shp: pl.BlockSpec(shp, lambda b, g: (0,) * len(shp))
    return pl.pallas_call(
        _cmp_body,
        grid=(B, G),
        in_specs=[pl.BlockSpec((None, None, M, CMP_STRIDE * LANES), lambda b, g: (b, g, 0, 0)),
                  const((CMP_LEN * LANES, CMP_HIDDEN)), const((CMP_LEN * LANES, CMP_HIDDEN)),
                  const((1, CMP_HIDDEN)), const((1, CMP_HIDDEN)),
                  const((CMP_HIDDEN, Dh)), const((CMP_HIDDEN, Dh))],
        out_specs=[pl.BlockSpec((None, None, M, LANES), lambda b, g: (b, g, 0, 0)),
                   pl.BlockSpec((None, None, Dh, M), lambda b, g: (b, g, 0, 0))],
        out_shape=[jax.ShapeDtypeStruct((B, G, M, LANES), bf16), jax.ShapeDtypeStruct((B, G, Dh, M), bf16)],
        compiler_params=_cparams("parallel", "parallel"),
        name="nsa_compress",
    )(x2, w1k_e, w1v_e, c0k, c0v, w2_k.astype(bf16), w2_v.astype(bf16))


def _nsa_body(q_ref, kc_ref, vct_ref, ks_ref, vst_ref, kw_ref, vwt_ref, gl_ref, ovt_ref,
              o_ref, score_ref, cnt_ref, q3_ref, selg_ref, s_sel, p_sel, acc_sel, s_win, p_win, acc_win):
    QT = NSA_QT
    KB = NSA_KB
    R = NSA_Q_PER_KV
    Dh = HEAD_DIM
    T = ks_ref.shape[0]
    NB = T // SEL_LEN
    NC = kc_ref.shape[0]
    qi = pl.program_id(2)
    t0 = qi * QT

    W = R * QT
    q3_ref[...] = jnp.concatenate([q_ref[r * LANES:(r + 1) * LANES, :] for r in range(R)], axis=1)
    slope_rows = q3_ref[Dh:Dh + NSA_AUG, :].astype(f32)[0:4]
    tq1 = t0 + lax.broadcasted_iota(jnp.int32, (1, QT), 1)
    tile3 = lambda t: jnp.concatenate([t] * R, axis=1)

    def col_reduce(x, op):
        parts = [x[i * SUBLANES:(i + 1) * SUBLANES] for i in range(x.shape[0] // SUBLANES)]
        while len(parts) > 1:
            nxt = [op(parts[2 * i], parts[2 * i + 1]) for i in range(len(parts) // 2)]
            parts = nxt + parts[2 * (len(parts) // 2):]
        return parts[0]

    colmax = lambda x: jnp.max(col_reduce(x, jnp.maximum), axis=0, keepdims=True)
    colsum = lambda x: jnp.sum(col_reduce(x, jnp.add), axis=0, keepdims=True)

    n_c = lax.broadcasted_iota(jnp.int32, (NC, 1), 0)
    vis_c = jnp.where(((n_c * CMP_STRIDE + (CMP_LEN - 1)) <= tq1) & (n_c < NC - 1), 0.0, NEG_INF)
    s_c = jnp.dot(kc_ref[...], q3_ref[...], preferred_element_type=f32) + tile3(vis_c)
    m_c = colmax(s_c)
    e_c = jnp.exp2(s_c - m_c)
    p_c = e_c * jnp.where(m_c > 0.5 * NEG_INF, 1.0 / colsum(e_c), 0.0)
    o_c = jnp.dot(vct_ref[...], p_c.astype(bf16), preferred_element_type=f32)
    p_sum = p_c[:, 0:QT]
    for r in range(1, R):
        p_sum = p_sum + p_c[:, r * QT:(r + 1) * QT]
    imp = _split_dot(p_sum, ovt_ref[...], 3, True)

    j = lax.broadcasted_iota(jnp.int32, (NB, 1), 0)
    cur = tq1 >> (SEL_LEN.bit_length() - 1)
    valid = (j * SEL_LEN) <= tq1
    forced = (j == 0) | (j == cur) | (j == cur - 1)
    score_ref[...] = jnp.where(valid, imp + FORCED_BONUS * forced.astype(f32), -jnp.inf)
    cnt_ref[...] = jnp.zeros_like(cnt_ref)
    nrb = NB // SUBLANES
    jsub = lax.broadcasted_iota(jnp.int32, (SUBLANES, 1), 0)
    last_rb = (t0 + QT - 1) // (SUBLANES * SEL_LEN)
    for ib in range(nrb):
        @pl.when(ib <= last_rb)
        def _(ib=ib):
            rows = [jnp.broadcast_to(score_ref[ib * SUBLANES + ii:ib * SUBLANES + ii + 1, :], (SUBLANES, QT))
                    for ii in range(SUBLANES)]
            for jb in range(nrb):
                @pl.when(jb <= last_rb)
                def _(jb=jb):
                    sj = score_ref[jb * SUBLANES:(jb + 1) * SUBLANES, :]
                    acc = cnt_ref[jb * SUBLANES:(jb + 1) * SUBLANES, :]
                    for ii in range(SUBLANES):
                        if ib < jb:
                            ahead = rows[ii] >= sj
                        elif ib > jb:
                            ahead = rows[ii] > sj
                        else:
                            ahead = (rows[ii] > sj) | ((rows[ii] == sj) & (ii < jsub))
                        acc = acc + jnp.where(ahead, 1.0, 0.0)
                    cnt_ref[jb * SUBLANES:(jb + 1) * SUBLANES, :] = acc
    selected = (cnt_ref[...] < float(min(SEL_TOPK, NB))) & valid
    sel_add = tile3(jnp.where(selected, 0.0, NEG_INF))
    spb = KB // SEL_LEN
    pad_rows = jnp.zeros((NSA_AUG - 4 - spb, W), f32)
    for tb in range(T // KB):
        selg_ref[tb] = jnp.concatenate([slope_rows, sel_add[tb * spb:(tb + 1) * spb], pad_rows], axis=0).astype(bf16)

    def stream(k_ref, vt_ref, n_plain, n_masked, start_of, mask_of, before_scores, s_buf, p_buf, acc_ref):
        n_all = n_plain + n_masked

        def scores(i, slot):
            ic = jnp.minimum(i, n_all - 1)
            before_scores(ic)
            s_buf[slot] = jnp.dot(k_ref[pl.ds(start_of(ic), KB), :], q3_ref[...], preferred_element_type=f32)

        def body(i, carry, masked, slot):
            m, l, alpha_prev = carry
            vblk = vt_ref[:, pl.ds(start_of(jnp.maximum(i - 1, 0)), KB)]
            pv = jnp.dot(vblk, p_buf[1 - slot], preferred_element_type=f32)
            scores(i + 1, 1 - slot)
            s = s_buf[slot]
            if masked:
                s = s + tile3(mask_of(i))
            m_new = jnp.maximum(m, colmax(s))
            alpha = jnp.exp2(m - m_new)
            p = jnp.exp2(s - m_new)
            p_buf[slot] = p.astype(bf16)
            acc_ref[...] = alpha_prev * acc_ref[...] + pv
            return m_new, alpha * l + colsum(p), alpha

        p_buf[1] = jnp.zeros((KB, W), bf16)
        acc_ref[...] = jnp.zeros_like(acc_ref)
        scores(0, 0)
        carry = (jnp.full((1, W), NEG_INF, f32), jnp.zeros((1, W), f32), jnp.ones((1, W), f32))

        def pair(j, carry, masked):
            return body(2 * j + 1, body(2 * j, carry, masked, 0), masked, 1)

        carry = lax.fori_loop(0, n_plain // 2, functools.partial(pair, masked=False), carry)
        _, l, alpha_last = lax.fori_loop(n_plain // 2, n_all // 2, functools.partial(pair, masked=True), carry)
        pv = jnp.dot(vt_ref[:, pl.ds(start_of(n_all - 1), KB)], p_buf[1], preferred_element_type=f32)
        return (alpha_last * acc_ref[...] + pv) * (1.0 / l)

    rel = lax.broadcasted_iota(jnp.int32, (1, QT), 1) - lax.broadcasted_iota(jnp.int32, (KB, 1), 0)

    nfull = qi * (QT // KB)

    def load_sel_rows(i):
        q3_ref[Dh:Dh + NSA_AUG, :] = selg_ref[i]

    o_s = stream(ks_ref, vst_ref, nfull, QT // KB,
                 lambda i: pl.multiple_of(i * KB, KB),
                 lambda i: jnp.where(rel - (i - nfull) * KB >= 0, 0.0, NEG_INF),
                 load_sel_rows, s_sel, p_sel, acc_sel)

    def win_mask(i):
        off = i * KB - WINDOW
        d = rel - off
        return jnp.where((d >= 0) & (d < WINDOW) & (t0 + off >= 0), 0.0, NEG_INF)

    o_w = stream(kw_ref, vwt_ref, 0, (WINDOW + QT) // KB,
                 lambda i: pl.multiple_of(jnp.maximum(t0 - WINDOW + i * KB, 0), KB),
                 win_mask, lambda i: None, s_win, p_win, acc_win)


    gates = _sigmoid(gl_ref[...])
    for r in range(R):
        ls = slice(r * QT, (r + 1) * QT)
        o_ref[r * Dh:(r + 1) * Dh, :] = (gates[3 * r:3 * r + 1, :] * o_c[:, ls]
                                          + gates[3 * r + 1:3 * r + 2, :] * o_s[:, ls]
                                          + gates[3 * r + 2:3 * r + 3, :] * o_w[:, ls]).astype(o_ref.dtype)


def _alibi_slopes(n):
    def pow2(m):
        start = 2.0 ** (-8.0 / m)
        return [start ** (i + 1) for i in range(m)]
    if math.log2(n).is_integer():
        return pow2(n)
    c = 2 ** math.floor(math.log2(n))
    return pow2(c) + pow2(2 * c)[0::2][: n - c]


def _nsa_attend(q_t, k_cmp, v_cmp_t, ks, vs_t, kw, vw_t, gl_t):
    B, G, T, _ = ks.shape
    Dh = HEAD_DIM
    QT = NSA_QT
    R = NSA_Q_PER_KV
    NB = T // SEL_LEN
    NC = k_cmp.shape[2]
    assert QT % NSA_KB == 0 and WINDOW % NSA_KB == 0 and T % QT == 0 and NSA_KB % SEL_LEN == 0
    n = np.arange(NC)
    jb = np.arange(NB)
    ov = ((n[:, None] * CMP_STRIDE < jb[None, :] * SEL_LEN + SEL_LEN)
          & (n[:, None] * CMP_STRIDE + CMP_LEN - 1 >= jb[None, :] * SEL_LEN) & (n[:, None] < NC - 1))
    ovt = jnp.asarray(ov.T.astype(np.float32))
    per_bg = lambda shp: pl.BlockSpec((None, None) + shp, lambda b, g, i: (b, g, 0, 0))
    const = lambda shp: pl.BlockSpec(shp, lambda b, g, i: (0, 0))
    return pl.pallas_call(
        _nsa_body,
        grid=(B, G, T // QT),
        in_specs=[pl.BlockSpec((None, R * LANES, QT), lambda b, g, i: (b, g, i)),
                  per_bg((NC, LANES)), per_bg((Dh, NC)),
                  per_bg((T, LANES)), per_bg((Dh, T)), per_bg((T, LANES)), per_bg((Dh, T)),
                  pl.BlockSpec((None, GL_ROWS, QT), lambda b, g, i: (b, g, i)),
                  const((NB, NC))],
        out_specs=pl.BlockSpec((None, R * Dh, QT), lambda b, g, i: (b, g, i)),
        out_shape=jax.ShapeDtypeStruct((B, G * R * Dh, T), bf16),
        scratch_shapes=[pltpu.VMEM((NB, QT), f32), pltpu.VMEM((NB, QT), f32), pltpu.VMEM((LANES, R * QT), bf16),
                        pltpu.VMEM((T // NSA_KB, NSA_AUG, R * QT), bf16),
                        pltpu.VMEM((2, NSA_KB, R * QT), f32), pltpu.VMEM((2, NSA_KB, R * QT), bf16),
                        pltpu.VMEM((Dh, R * QT), f32),
                        pltpu.VMEM((2, NSA_KB, R * QT), f32), pltpu.VMEM((2, NSA_KB, R * QT), bf16),
                        pltpu.VMEM((Dh, R * QT), f32)],
        compiler_params=_cparams("parallel", "parallel", "arbitrary"),
        name="nsa_attend",
    )(q_t, k_cmp, v_cmp_t, ks, vs_t, kw, vw_t, gl_t, ovt)


def _s5_body(u_ref, tg_ref, bend_ref, cpow_ref, are_ref, aim_ref, y_ref, start_ref, inc_ref):
    nk = u_ref.shape[0] // SUBLANES
    u = u_ref[...]
    inc_ref[...] = jnp.dot(u, bend_ref[...], preferred_element_type=f32)
    are = are_ref[...]
    aim = aim_ref[...]

    def step(kc, st):
        r0 = pl.multiple_of(kc * SUBLANES, SUBLANES)
        start_ref[pl.ds(r0, SUBLANES), :] = st
        return are * st + aim * pltpu.roll(st, S5_STATE, 1) + inc_ref[pl.ds(r0, SUBLANES), :]

    lax.fori_loop(0, nk, step, jnp.zeros((SUBLANES, 2 * S5_STATE), f32))
    y_ref[...] = (jnp.dot(u, tg_ref[...], preferred_element_type=f32)
                  + jnp.dot(start_ref[...].astype(bf16), cpow_ref[...], preferred_element_type=f32))


def _s5_tables(lam_re, lam_im, log_dt, b_re, b_im, c_re, c_im):
    Cs = S5_CHUNK
    lam_re, lam_im = lam_re.astype(f32), lam_im.astype(f32)
    b_re, b_im, c_re, c_im = (t.astype(f32) for t in (b_re, b_im, c_re, c_im))
    dt = jnp.exp(log_dt.astype(f32))[:, None]
    mag = jnp.exp(lam_re * dt)
    ab_re, ab_im = mag * jnp.cos(lam_im * dt), mag * jnp.sin(lam_im * dt)
    den = lam_re * lam_re + lam_im * lam_im
    f_re = ((ab_re - 1.0) * lam_re + ab_im * lam_im) / den
    f_im = (ab_im * lam_re - (ab_re - 1.0) * lam_im) / den
    bb_re = f_re[..., None] * b_re - f_im[..., None] * b_im
    bb_im = f_re[..., None] * b_im + f_im[..., None] * b_re
    tau = jnp.arange(Cs + 1, dtype=f32)[None, None, :]
    pmag = jnp.exp(lam_re[..., None] * dt[..., None] * tau)
    pang = lam_im[..., None] * dt[..., None] * tau
    pw_re, pw_im = pmag * jnp.cos(pang), pmag * jnp.sin(pang)
    ein = functools.partial(jnp.einsum, precision=HI)
    ca_re = c_re[..., None] * pw_re[:, None] - c_im[..., None] * pw_im[:, None]
    ca_im = c_re[..., None] * pw_im[:, None] + c_im[..., None] * pw_re[:, None]
    kern = ein('gopt,gpi->gtoi', ca_re[..., :Cs], bb_re) - ein('gopt,gpi->gtoi', ca_im[..., :Cs], bb_im)
    s_idx = jnp.arange(Cs)
    lag_np = np.arange(Cs)[None, :] - np.arange(Cs)[:, None]
    place = jnp.asarray((lag_np[:, :, None] == np.arange(Cs)[None, None, :]).astype(np.float32))
    tg = ein('stl,gloi->gsito', place, kern)
    tg = tg.reshape(S5_GROUPS, Cs * S5_CH, Cs * S5_CH)
    rev = (Cs - 1) - s_idx
    pe_re, pe_im = pw_re[:, :, rev], pw_im[:, :, rev]
    be_re = pe_re[..., None] * bb_re[:, :, None, :] - pe_im[..., None] * bb_im[:, :, None, :]
    be_im = pe_re[..., None] * bb_im[:, :, None, :] + pe_im[..., None] * bb_re[:, :, None, :]
    bend = jnp.concatenate([jnp.transpose(be_re, (0, 2, 3, 1)), jnp.transpose(be_im, (0, 2, 3, 1))], axis=-1)
    bend = bend.reshape(S5_GROUPS, Cs * S5_CH, 2 * S5_STATE)
    cp_re = jnp.transpose(ca_re[..., 1:], (0, 2, 3, 1))
    cp_im = jnp.transpose(ca_im[..., 1:], (0, 2, 3, 1))
    cpow = jnp.concatenate([cp_re, -cp_im], axis=1).reshape(S5_GROUPS, 2 * S5_STATE, Cs * S5_CH)
    a_re = jnp.concatenate([pw_re[..., Cs], pw_re[..., Cs]], axis=-1)[:, None, :]
    a_im = jnp.concatenate([-pw_im[..., Cs], pw_im[..., Cs]], axis=-1)[:, None, :]
    return tg.astype(bf16), bend.astype(bf16), cpow.astype(bf16), a_re, a_im


def _s5_scan(z_s5, tables):
    B, T, _ = z_s5.shape
    assert B == SUBLANES
    Cs = S5_CHUNK
    nk = T // Cs
    tg, bend, cpow, a_re, a_im = tables
    u = z_s5.reshape(B, nk, Cs, S5_GROUPS, S5_CH)
    u = jnp.transpose(u, (3, 1, 0, 2, 4)).reshape(S5_GROUPS, nk * B, Cs * S5_CH).astype(bf16)
    W = Cs * S5_CH
    per_g = lambda shp: pl.BlockSpec((None,) + shp, lambda g: (g, 0, 0))
    y = pl.pallas_call(
        _s5_body,
        grid=(S5_GROUPS,),
        in_specs=[per_g((nk * B, W)), per_g((W, W)), per_g((W, 2 * S5_STATE)), per_g((2 * S5_STATE, W)),
                  per_g((1, 2 * S5_STATE)), per_g((1, 2 * S5_STATE))],
        out_specs=per_g((nk * B, W)),
        out_shape=jax.ShapeDtypeStruct((S5_GROUPS, nk * B, W), f32),
        scratch_shapes=[pltpu.VMEM((nk * B, 2 * S5_STATE), f32), pltpu.VMEM((nk * B, 2 * S5_STATE), f32)],
        compiler_params=_cparams("parallel"),
        name="s5_conv",
    )(u, tg, bend, cpow, a_re, a_im)
    y = y.reshape(S5_GROUPS, nk, B, Cs, S5_CH)
    return jnp.transpose(y, (2, 1, 3, 0, 4)).reshape(B, T, S5_W)


def _s5_post_body(y_ref, u_ref, d_ref, w_ref, o_ref):
    y = _gelu_tanh(y_ref[...] + d_ref[...] * u_ref[...])
    vg = _dot(y, w_ref[...])
    o_ref[...] = vg[:, 0:S5_W] * _sigmoid(vg[:, S5_W:2 * S5_W])


def _s5_post(y2d, u2d, d_skip, w_glu):
    n = y2d.shape[0]
    tm = min(S5P_TM, n)
    blk = pl.BlockSpec((tm, S5_W), lambda i: (i, 0))
    return pl.pallas_call(
        _s5_post_body,
        grid=(n // tm,),
        in_specs=[blk, blk, pl.BlockSpec((1, S5_W), lambda i: (0, 0)),
                  pl.BlockSpec((S5_W, 2 * S5_W), lambda i: (0, 0))],
        out_specs=blk,
        out_shape=jax.ShapeDtypeStruct((n, S5_W), f32),
        compiler_params=_cparams("parallel"),
        name="s5_glu",
    )(y2d, u2d, d_skip.reshape(1, S5_W), w_glu.astype(bf16))


def _out_proj_body(h_ref, orw_ref, onsat_ref, os5_ref, w_ref, g_ref, o_ref):
    mix = (_dot(orw_ref[...], w_ref[0:RWKV_W, :])
           + _dot_tn(onsat_ref[...], w_ref[RWKV_W:RWKV_W + NSA_W, :])
           + _dot(os5_ref[...], w_ref[RWKV_W + NSA_W:D_MIX, :]))
    o_ref[...] = h_ref[...] + _rms(mix, g_ref[...])


def _out_proj(h, o_rw, o_nsa_t, o_s5, w_out, g):
    B, T, _ = h.shape
    tm = min(OUT_TM, T)
    rows = lambda w: pl.BlockSpec((None, tm, w), lambda b, i: (b, i, 0))
    return pl.pallas_call(
        _out_proj_body,
        grid=(B, T // tm),
        in_specs=[rows(D_MODEL), rows(RWKV_W), pl.BlockSpec((None, NSA_W, tm), lambda b, i: (b, 0, i)), rows(S5_W),
                  pl.BlockSpec((D_MIX, D_MODEL), lambda b, i: (0, 0)),
                  pl.BlockSpec((1, D_MODEL), lambda b, i: (0, 0))],
        out_specs=rows(D_MODEL),
        out_shape=jax.ShapeDtypeStruct((B, T, D_MODEL), f32),
        compiler_params=_cparams("parallel", "parallel"),
        name="out_proj",
    )(h, o_rw, o_nsa_t, o_s5, w_out.astype(bf16), g.reshape(1, D_MODEL))


def _ffn_body(h_ref, hp_ref, gpre_ref, wg_ref, wu_ref, cwg_ref, cwu_ref, cbg_ref, cbu_ref, wd_ref, gpost_ref,
              o_ref, xn_ref, acc_ref, *, tiles_per_seq):
    tm = h_ref.shape[0]
    H = SUBLANES
    c = pl.program_id(1)

    @pl.when(c == 0)
    def _():
        first = (pl.program_id(0) % tiles_per_seq) == 0
        xp = _rms(hp_ref[...], gpre_ref[...])
        xn_ref[0:H, :] = jnp.where(first, 0.0, xp).astype(bf16)
        xn_ref[H:H + tm, :] = _rms(h_ref[...], gpre_ref[...]).astype(bf16)
        acc_ref[...] = jnp.zeros_like(acc_ref)

    xn = xn_ref[...]

    def conv_branch(w_ref, cw_ref, cb_ref):
        hu = jnp.dot(xn, w_ref[...], preferred_element_type=f32)
        cw = cw_ref[...]
        out = (cw[0:1, :] * pltpu.roll(hu, 2, 0)[H:H + tm]
               + cw[1:2, :] * pltpu.roll(hu, 1, 0)[H:H + tm]
               + cw[2:3, :] * hu[H:H + tm])
        return out + cb_ref[...]

    gate = conv_branch(wg_ref, cwg_ref, cbg_ref)
    up = conv_branch(wu_ref, cwu_ref, cbu_ref)
    acc_ref[...] += _dot(_gelu_tanh(gate) * up, wd_ref[...])

    @pl.when(c == pl.num_programs(1) - 1)
    def _():
        o_ref[...] = h_ref[...] + _rms(acc_ref[...], gpost_ref[...])


def _conv_ffn(h2d, seq_len, g_pre, w_up, conv_w, conv_b, w_down, g_post):
    n = h2d.shape[0]
    tm = min(FFN_TM, seq_len)
    tc = FFN_TC
    nc = D_FF // tc
    H = SUBLANES
    tiles_per_seq = seq_len // tm
    hpb = tm // H
    w_up = w_up.astype(bf16)
    cb = conv_b.reshape(1, 2 * D_FF)
    return pl.pallas_call(
        functools.partial(_ffn_body, tiles_per_seq=tiles_per_seq),
        grid=(n // tm, nc),
        in_specs=[pl.BlockSpec((tm, D_MODEL), lambda i, c: (i, 0)),
                  pl.BlockSpec((H, D_MODEL), lambda i, c: (jnp.maximum(i * hpb - 1, 0), 0)),
                  pl.BlockSpec((1, D_MODEL), lambda i, c: (0, 0)),
                  pl.BlockSpec((D_MODEL, tc), lambda i, c: (0, c)),
                  pl.BlockSpec((D_MODEL, tc), lambda i, c: (0, nc + c)),
                  pl.BlockSpec((CONV_W, tc), lambda i, c: (0, c)),
                  pl.BlockSpec((CONV_W, tc), lambda i, c: (0, nc + c)),
                  pl.BlockSpec((1, tc), lambda i, c: (0, c)),
                  pl.BlockSpec((1, tc), lambda i, c: (0, nc + c)),
                  pl.BlockSpec((tc, D_MODEL), lambda i, c: (c, 0)),
                  pl.BlockSpec((1, D_MODEL), lambda i, c: (0, 0))],
        out_specs=pl.BlockSpec((tm, D_MODEL), lambda i, c: (i, 0)),
        out_shape=jax.ShapeDtypeStruct((n, D_MODEL), f32),
        scratch_shapes=[pltpu.VMEM((H + tm, D_MODEL), bf16), pltpu.VMEM((tm, D_MODEL), f32)],
        compiler_params=_cparams("parallel", "arbitrary"),
        name="conv_ffn",
    )(h2d, h2d, g_pre.reshape(1, D_MODEL), w_up, w_up, conv_w, conv_w, cb, cb, w_down.astype(bf16),
      g_post.reshape(1, D_MODEL))


def _ple_body(h_ref, p_ref, wp_ref, gp_ref, wg_ref, o_ref):
    h = h_ref[...]
    e = _rms(_dot(p_ref[...], wp_ref[...]), gp_ref[...])
    gate = _sigmoid(_dot(h, wg_ref[...]))
    o_ref[...] = h + gate * e


def _ple(h2d, p2d, w_ple, g_ple, w_gate):
    n = h2d.shape[0]
    tm = min(PLE_TM, n)
    return pl.pallas_call(
        _ple_body,
        grid=(n // tm,),
        in_specs=[pl.BlockSpec((tm, D_MODEL), lambda i: (i, 0)),
                  pl.BlockSpec((tm, PLE_DIM), lambda i: (i, 0)),
                  pl.BlockSpec((PLE_DIM, D_MODEL), lambda i: (0, 0)),
                  pl.BlockSpec((1, D_MODEL), lambda i: (0, 0)),
                  pl.BlockSpec((D_MODEL, D_MODEL), lambda i: (0, 0))],
        out_specs=pl.BlockSpec((tm, D_MODEL), lambda i: (i, 0)),
        out_shape=jax.ShapeDtypeStruct((n, D_MODEL), f32),
        compiler_params=_cparams("parallel"),
        name="ple_gate",
    )(h2d, p2d, w_ple.astype(bf16), g_ple.reshape(1, D_MODEL), w_gate.astype(bf16))


def kernel(x, p, pre_mix_norm, post_mix_norm, pre_ffn_norm, post_ffn_norm, w_in, w_out, shift_mu, rw_w0, rw_w2, rw_a0, rw_a2, rw_g2, rw_k_k, rw_k_a, rw_r_k, rw_gn_w, rw_gn_b, cmp_pe_k, cmp_pe_v, cmp_w1_k, cmp_w2_k, cmp_w1_v, cmp_w2_v, s5_lam_re, s5_lam_im, s5_log_dt, s5_b_re, s5_b_im, s5_c_re, s5_c_im, s5_d, s5_w_glu, w_up, conv_w, conv_b, w_down, w_ple, ple_norm, w_ple_gate):
    B, T, D = x.shape
    n = B * T
    depth = w_in.shape[0]
    h = x
    for i in range(depth):
        wn, wt, qaug, mu_p = _prep_in_weights(w_in[i], shift_mu[i])
        z_rw, z_s5, ks, kw, kcv, q_t, vs_t, vw_t, gl_t = _in_proj(h, pre_mix_norm[i], wn, wt, qaug)
        o_rw = _rwkv(z_rw, mu_p, rw_w0[i], rw_w2[i], rw_a0[i], rw_a2[i], rw_g2[i],
                     rw_k_k[i], rw_k_a[i], rw_r_k[i], rw_gn_w[i], rw_gn_b[i])
        k_cmp, v_cmp_t = _compress(kcv, cmp_pe_k[i], cmp_pe_v[i], cmp_w1_k[i], cmp_w2_k[i], cmp_w1_v[i], cmp_w2_v[i])
        o_nsa_t = _nsa_attend(q_t, k_cmp, v_cmp_t, ks, vs_t, kw, vw_t, gl_t)
        tables = _s5_tables(s5_lam_re[i], s5_lam_im[i], s5_log_dt[i], s5_b_re[i], s5_b_im[i], s5_c_re[i], s5_c_im[i])
        y_s5 = _s5_scan(z_s5, tables)
        o_s5 = _s5_post(y_s5.reshape(n, S5_W), z_s5.reshape(n, S5_W), s5_d[i], s5_w_glu[i])
        h = _out_proj(h, o_rw, o_nsa_t, o_s5.reshape(B, T, S5_W), w_out[i], post_mix_norm[i])
        h = _conv_ffn(h.reshape(n, D), T, pre_ffn_norm[i], w_up[i], conv_w[i], conv_b[i], w_down[i], post_ffn_norm[i])
        h = _ple(h, p[i].reshape(n, PLE_DIM), w_ple[i], ple_norm[i], w_ple_gate[i]).reshape(B, T, D)
    return h
```

```python
import functools
import math

import numpy as np
import jax
import jax.numpy as jnp
from jax import lax
from jax.experimental import pallas as pl
from jax.experimental.pallas import tpu as pltpu

f32 = jnp.float32
bf16 = jnp.bfloat16
HI = lax.Precision.HIGHEST

D_MODEL = 1024
HEAD_DIM = 64
RWKV_HEADS = 6
RWKV_W = RWKV_HEADS * HEAD_DIM
D_DECAY_LORA = 64
D_AAA_LORA = 64
D_GATE_LORA = 128
RWKV_COLS = 3 * RWKV_W + D_DECAY_LORA + D_AAA_LORA + D_GATE_LORA
GN_EPS = 64e-5
NSA_Q_HEADS = 6
NSA_KV_GROUPS = 2
NSA_Q_PER_KV = NSA_Q_HEADS // NSA_KV_GROUPS
NSA_W = NSA_Q_HEADS * HEAD_DIM
NSA_KV_W = NSA_KV_GROUPS * HEAD_DIM
NSA_N_BRANCH = 3
NSA_COLS = NSA_W + 6 * NSA_KV_W + NSA_Q_HEADS * NSA_N_BRANCH
CMP_LEN = 32
CMP_STRIDE = 16
CMP_HIDDEN = 128
SEL_LEN = 64
SEL_TOPK = 16
WINDOW = 512
FORCED_BONUS = 1e3
NEG_INF = -1e30
S5_GROUPS = 16
S5_CH = 16
S5_W = S5_GROUPS * S5_CH
S5_STATE = 64
D_MIX = RWKV_W + NSA_W + S5_W
D_FF = 2816
CONV_W = 3
PLE_DIM = 256
NORM_EPS = 1e-6

LANES = 128
SUBLANES = 8
VMEM_LIMIT = 56 * 1024 * 1024

RW_LORA_W_OFF = 3 * RWKV_W
RW_LORA_A_OFF = RW_LORA_W_OFF + LANES
RW_LORA_G_OFF = RW_LORA_A_OFF + LANES
RW_P = RW_LORA_G_OFF + D_GATE_LORA
NSA_GATE_OFF = NSA_W + 6 * NSA_KV_W
NAT_KSW = RW_P
NAT_KCV = NAT_KSW + NSA_KV_GROUPS * 2 * LANES
NAT_P = NAT_KCV + NSA_KV_GROUPS * LANES
GL_ROWS = 16
TR_VS = NSA_Q_HEADS * LANES
TR_VW = TR_VS + NSA_KV_W
TR_GL = TR_VW + NSA_KV_W
TR_S5 = TR_GL + NSA_KV_GROUPS * GL_ROWS
TR_P = TR_S5 + S5_W

IN_TM = 256
RW_CHUNK = 64
RW_NCH = 4
NSA_QT = 512
NSA_KB = 256
NSA_AUG = 16
S5_CHUNK = 64
S5_SCAN_LEVELS = 8
OUT_TM = 512
FFN_TM = 512
FFN_TC = 1408
PLE_TM = 512
S5P_TM = 1024


def _cparams(*sem):
    return pltpu.CompilerParams(dimension_semantics=sem, vmem_limit_bytes=VMEM_LIMIT)


def _rms(x, g):
    ms = jnp.mean(x * x, axis=-1, keepdims=True)
    return x * lax.rsqrt(ms + NORM_EPS) * g


def _gelu_tanh(x):
    return 0.5 * x * (1.0 + jnp.tanh(math.sqrt(2.0 / math.pi) * (x + 0.044715 * (x * x * x))))


def _sigmoid(x):
    return 1.0 / (1.0 + jnp.exp(-x))


def _softplus(x):
    return jnp.maximum(x, 0.0) + jnp.log(1.0 + jnp.exp(-jnp.abs(x)))


def _dot(a, b):
    return jnp.dot(a.astype(bf16), b.astype(bf16), preferred_element_type=f32)


def _dot_nt(a, b):
    return lax.dot_general(a.astype(bf16), b.astype(bf16), (((1,), (1,)), ((), ())), preferred_element_type=f32)


def _dot_tn(a, b):
    return lax.dot_general(a.astype(bf16), b.astype(bf16), (((0,), (0,)), ((), ())), preferred_element_type=f32)


def _dot_hi(a, b):
    return jnp.dot(a, b, preferred_element_type=f32, precision=HI)


def _split_dot(x, m01, terms, m_left):
    m = m01.astype(bf16)
    acc = None
    rem = x
    for _ in range(terms):
        piece = rem.astype(bf16)
        d = (jnp.dot(m, piece, preferred_element_type=f32) if m_left
             else jnp.dot(piece, m, preferred_element_type=f32))
        acc = d if acc is None else acc + d
        rem = rem - piece.astype(f32)
    return acc


def _key_tail(t, sel_onehot):
    col = lax.broadcasted_iota(jnp.int32, t.shape[:1] + (LANES,), 1) - HEAD_DIM
    lo = (t & (LANES - 1)).astype(f32)
    hi = (t >> (LANES.bit_length() - 1)).astype(f32)
    tail = jnp.where((col == 0) | (col == 2), lo, jnp.where((col == 1) | (col == 3), hi, 0.0))
    if sel_onehot:
        spb = NSA_KB // SEL_LEN
        blk = (t >> (SEL_LEN.bit_length() - 1)) & (spb - 1)
        tail = tail + jnp.where(col - 4 == blk, 1.0, 0.0) * jnp.where((col >= 4) & (col < 4 + spb), 1.0, 0.0)
    return tail


def _in_proj_body(x_ref, g_ref, wn_ref, wt_ref, qaug_ref,
                  zrw_ref, zs5_ref, ks_ref, kw_ref, kcv_ref, qt_ref, vst_ref, vwt_ref, glt_ref):
    tm = x_ref.shape[0]
    G, Dh = NSA_KV_GROUPS, HEAD_DIM
    xn = _rms(x_ref[...], g_ref[...]).astype(bf16)
    zrw_ref[...] = jnp.dot(xn, wn_ref[:, 0:RW_P], preferred_element_type=f32)
    ksw =jnp.dot(xn, wn_ref[:, NAT_KSW:NAT_KCV], preferred_element_type=f32)
    tok = pl.program_id(1) * tm + lax.broadcasted_iota(jnp.int32, (tm, 1), 0)
    tail_w = _key_tail(tok, False)
    tail_s = _key_tail(tok, True)
    for g in range(G):
        ks_ref[g] = (ksw[:, 2 * g * LANES:(2 * g + 1) * LANES] + tail_s).astype(bf16)
        kw_ref[g] = (ksw[:, (2 * g + 1) * LANES:(2 * g + 2) * LANES] + tail_w).astype(bf16)
    kcv = jnp.dot(xn, wn_ref[:, NAT_KCV:NAT_P], preferred_element_type=f32)
    for g in range(G):
        kcv_ref[g] = kcv[:, g * LANES:(g + 1) * LANES].astype(bf16)
    zt = lax.dot_general(wt_ref[...], xn, (((1,), (1,)), ((), ())), preferred_element_type=f32)
    qscale = HEAD_DIM ** -0.5 * math.log2(math.e)
    qt_ref[...] = (zt[0:TR_VS] * qscale + qaug_ref[...]).astype(bf16)
    for g in range(G):
        vst_ref[g] = zt[TR_VS + g * Dh:TR_VS + (g + 1) * Dh].astype(bf16)
        vwt_ref[g] = zt[TR_VW + g * Dh:TR_VW + (g + 1) * Dh].astype(bf16)
    glt_ref[...] = zt[TR_GL:TR_S5]
    zs5_ref[...] = zt[TR_S5:TR_P]


def _in_proj(h, g, wn, wt, qaug):
    B, T, _ = h.shape
    tm = min(IN_TM, T)
    G, Dh = NSA_KV_GROUPS, HEAD_DIM
    const = lambda shp: pl.BlockSpec(shp, lambda b, i: (0,) * len(shp))
    tok3 = lambda w: pl.BlockSpec((None, tm, w), lambda b, i: (b, i, 0))
    tok4 = lambda w: pl.BlockSpec((None, G, tm, w), lambda b, i: (b, 0, i, 0))
    trn3 = lambda r: pl.BlockSpec((None, r, tm), lambda b, i: (b, 0, i))
    trn4 = pl.BlockSpec((None, G, Dh, tm), lambda b, i: (b, 0, 0, i))
    S = jax.ShapeDtypeStruct
    return pl.pallas_call(
        _in_proj_body,
        grid=(B, T // tm),
        in_specs=[tok3(D_MODEL), const((1, D_MODEL)), const((D_MODEL, NAT_P)), const((TR_P, D_MODEL)),
                  const((TR_VS, 1))],
        out_specs=[tok3(RW_P), trn3(S5_W), tok4(LANES), tok4(LANES), tok4(LANES),
                   trn3(TR_VS), trn4, trn4, trn3(TR_S5 - TR_GL)],
        out_shape=[S((B, T, RW_P), f32), S((B, S5_W, T), f32), S((B, G, T, LANES), bf16), S((B, G, T, LANES), bf16),
                   S((B, G, T, LANES), bf16), S((B, TR_VS, T), bf16), S((B, G, Dh, T), bf16), S((B, G, Dh, T), bf16),
                   S((B, TR_S5 - TR_GL, T), f32)],
        compiler_params=_cparams("parallel", "parallel"),
        name="in_proj",
    )(h, g.reshape(1, D_MODEL), wn, wt, qaug)


def _prep_in_weights(w_in, shift_mu):
    d = w_in.shape[0]
    G, R, Dh = NSA_KV_GROUPS, NSA_Q_PER_KV, HEAD_DIM
    zc = lambda k: jnp.zeros((d, k), w_in.dtype)
    o = 3 * RWKV_W
    rw = [w_in[:, :o],
          w_in[:, o:o + D_DECAY_LORA], zc(LANES - D_DECAY_LORA),
          w_in[:, o + D_DECAY_LORA:o + D_DECAY_LORA + D_AAA_LORA], zc(LANES - D_AAA_LORA),
          w_in[:, o + D_DECAY_LORA + D_AAA_LORA:RWKV_COLS]]
    nsa = w_in[:, RWKV_COLS:RWKV_COLS + NSA_COLS]
    s5 = w_in[:, RWKV_COLS + NSA_COLS:]
    sec = lambda i, g: nsa[:, NSA_W + i * NSA_KV_W + g * Dh:NSA_W + i * NSA_KV_W + (g + 1) * Dh]
    ksw = [t for g in range(G) for t in (sec(2, g), zc(LANES - Dh), sec(4, g), zc(LANES - Dh))]
    kcv = [t for g in range(G) for t in (sec(0, g), sec(1, g))]
    wn = jnp.concatenate(rw + ksw + kcv, axis=1).astype(bf16)
    q_cols = [t for h in range(G * R) for t in (nsa[:, h * Dh:(h + 1) * Dh], zc(LANES - Dh))]
    nb = R * NSA_N_BRANCH
    gl_cols = [t for g in range(G) for t in (nsa[:, NSA_GATE_OFF + g * nb:NSA_GATE_OFF + (g + 1) * nb],
                                             zc(GL_ROWS - nb))]
    wt = jnp.concatenate(q_cols + [sec(3, g) for g in range(G)] + [sec(5, g) for g in range(G)] + gl_cols + [s5],
                         axis=1)
    wt = wt.T.astype(bf16)
    sl = np.asarray(_alibi_slopes(NSA_Q_HEADS), np.float64) * math.log2(math.e)
    sl_hi = sl.astype(bf16).astype(np.float64)
    sl_lo = sl - sl_hi
    qaug = np.zeros((G * R, LANES), np.float32)
    qaug[:, Dh:Dh + 4] = np.stack([sl_hi, sl_hi * LANES, sl_lo, sl_lo * LANES], axis=-1)
    z1 = lambda k: jnp.zeros((k,), shift_mu.dtype)
    mu_p = jnp.concatenate([shift_mu[:o],
                            shift_mu[o:o + D_DECAY_LORA], z1(LANES - D_DECAY_LORA),
                            shift_mu[o + D_DECAY_LORA:o + D_DECAY_LORA + D_AAA_LORA], z1(LANES - D_AAA_LORA),
                            shift_mu[o + D_DECAY_LORA + D_AAA_LORA:]])
    return wn, wt, jnp.asarray(qaug.reshape(TR_VS, 1)), mu_p.reshape(1, RW_P)


def _rwkv_body(z_ref, mu_ref, w0_ref, w2_ref, a0_ref, a2_ref, g2_ref, kk_ref, ka_ref, rk_ref, gnw_ref, gnb_ref,
               o_ref, state_ref, prev_ref):
    C = RW_CHUNK
    N = HEAD_DIM
    L = z_ref.shape[0]
    nch = L // C
    npair = RWKV_W // LANES

    @pl.when(pl.program_id(1) == 0)
    def _():
        state_ref[...] = jnp.zeros_like(state_ref)
        prev_ref[...] = jnp.zeros_like(prev_ref)

    z = z_ref[...]
    row = lax.broadcasted_iota(jnp.int32, z.shape, 0)
    zprev = jnp.where(row == 0, prev_ref[0:1, :], pltpu.roll(z, 1, 0))
    prev_ref[0:1, :] = z[L - 1:L, :]
    zs = z + (zprev - z) * mu_ref[...]

    r = zs[:, 0:RWKV_W]
    k = zs[:, RWKV_W:2 * RWKV_W]
    v = zs[:, 2 * RWKV_W:3 * RWKV_W]
    wl = zs[:, RW_LORA_W_OFF:RW_LORA_W_OFF + LANES]
    al = zs[:, RW_LORA_A_OFF:RW_LORA_A_OFF + LANES]
    gl = zs[:, RW_LORA_G_OFF:RW_LORA_G_OFF + D_GATE_LORA]
    lane = lax.broadcasted_iota(jnp.int32, (1, LANES), 1)
    li = lax.broadcasted_iota(jnp.int32, (LANES, LANES), 0)
    lj = lax.broadcasted_iota(jnp.int32, (LANES, LANES), 1)
    hshift = N.bit_length() - 1
    bd128 = ((li >> hshift) == (lj >> hshift)).astype(f32)
    eye128 = (li == lj).astype(f32)
    first_half = lane < N
    lane3 = lax.broadcasted_iota(jnp.int32, (1, RWKV_W), 1)
    half_mask = [(((lane3 >> hshift) & 1) == hh).astype(f32) for hh in range(2)]

    def head_sums(x):
        return jnp.concatenate([_split_dot(x[:, j * LANES:(j + 1) * LANES], bd128, 2, False)
                                for j in range(npair)], axis=1)

    w = -_softplus(-(w0_ref[...] + _dot(jnp.tanh(wl), w2_ref[...]))) - 0.5
    logd = -jnp.exp(w)
    a = _sigmoid(a0_ref[...] + _dot(al, a2_ref[...]))
    g = _dot(_sigmoid(gl), g2_ref[...])
    kkr = k * kk_ref[...]
    kk = kkr / jnp.maximum(jnp.sqrt(head_sums(kkr * kkr)), 1e-12)
    kmod = k * (1.0 + (a - 1.0) * ka_ref[...])
    b = kk * a

    ti = lax.broadcasted_iota(jnp.int32, (C, C), 0)
    si = lax.broadcasted_iota(jnp.int32, (C, C), 1)
    tri_incl = (si <= ti).astype(f32)
    tri_strict = (si < ti).astype(f32)
    eye = (si == ti).astype(f32)
    cums, tots = [], []
    for c in range(nch):
        cc = _split_dot(logd[c * C:(c + 1) * C], tri_incl, 3, True)
        cums.append(cc)
        tots.append(jnp.broadcast_to(cc[C - 1:C], (C, RWKV_W)))
    cum = jnp.concatenate(cums, axis=0)
    tot = jnp.concatenate(tots, axis=0)
    e_inv = jnp.exp(-cum)
    e_end = jnp.exp(tot - cum)
    r_t = r * jnp.exp(cum)
    kk_t = kk * jnp.exp(cum - logd)
    k_h = kmod * e_inv
    b_h = b * e_inv
    k_e = kmod * e_end
    b_e = b * e_end
    e_tot = jnp.exp(tot)
    kk_t_m = [kk_t * m for m in half_mask]
    r_t_m = [r_t * m for m in half_mask]
    k_e_m = [k_e * m for m in half_mask]
    b_e_m = [b_e * m for m in half_mask]

    lvl_masks = []
    s = 1
    while s < C:
        sh = s.bit_length() - 1
        m = ((ti >> (sh + 1)) == (si >> (sh + 1))) & (((ti >> sh) & 1) == 1) & (((si >> sh) & 1) == 0)
        lvl_masks.append(m.astype(f32))
        s *= 2

    chains = [(c, h) for c in range(nch) for h in range(RWKV_HEADS)]

    def blk(arr, c, h):
        j = h // 2
        return arr[c * C:(c + 1) * C, j * LANES:(j + 1) * LANES]

    p = [_dot_nt(jnp.concatenate([blk(kk_t_m[h % 2], c, h), blk(r_t_m[h % 2], c, h)], axis=0),
                 jnp.concatenate([blk(b_h, c, h), blk(k_h, c, h)], axis=0)) for c, h in chains]
    a_ab = [t[0:C, 0:C] * tri_strict for t in p]
    a_ak = [t[0:C, C:2 * C] * tri_strict for t in p]
    a_rb = [t[C:2 * C, 0:C] * tri_incl for t in p]
    a_rk = [t[C:2 * C, C:2 * C] * tri_incl for t in p]
    x = [eye - t * lvl_masks[0] for t in a_ab]
    for m in lvl_masks[1:]:
        t1 = [_dot(xg, ag * m) for xg, ag in zip(x, a_ab)]
        x = [xg - _dot(tg, xg) for xg, tg in zip(x, t1)]
    av = [_dot(jnp.concatenate([a_ak[i], a_rk[i]], axis=0), blk(v, c, h)) for i, (c, h) in enumerate(chains)]
    wu = [_dot(x[i], jnp.concatenate([blk(kk_t, c, h), av[i][0:C]], axis=1)) for i, (c, h) in enumerate(chains)]
    bwu = [_dot_tn(blk(b_e_m[h % 2], c, h), wu[i]) for i, (c, h) in enumerate(chains)]
    ktv = [_dot_tn(blk(k_e_m[h % 2], c, h), blk(v, c, h)) for c, h in chains]
    arb = [_dot(a_rb[i], wu[i]) for i in range(len(chains))]

    state = [state_ref[j] for j in range(npair)]
    out_rows = []
    for c in range(nch):
        out_lanes = []
        for j in range(npair):
            ge = c * RWKV_HEADS + 2 * j
            go = ge + 1
            bsum = bwu[ge] + bwu[go]
            tm = eye128 * e_tot[c * C:c * C + 1, j * LANES:(j + 1) * LANES] - bsum[:, 0:LANES] * bd128
            zc = (ktv[ge] + ktv[go] - bsum[:, LANES:2 * LANES]) * bd128
            q = blk(r_t, c, 2 * j) - jnp.where(first_half, arb[ge][:, 0:LANES], arb[go][:, 0:LANES])
            o_in = jnp.where(first_half, av[ge][C:2 * C] - arb[ge][:, LANES:2 * LANES],
                             av[go][C:2 * C] - arb[go][:, LANES:2 * LANES])
            ts = _dot(jnp.concatenate([tm, q], axis=0), state[j])
            state[j] = ts[0:LANES] + zc
            out_lanes.append(ts[LANES:LANES + C] + o_in)
        out_rows.append(jnp.concatenate(out_lanes, axis=1))
    for j in range(npair):
        state_ref[j] = state[j]
    o = jnp.concatenate(out_rows, axis=0)

    inv_n = 1.0 / N
    mean = head_sums(o) * inv_n
    oc = o - mean
    var = head_sums(oc * oc) * inv_n
    on = oc * lax.rsqrt(var + GN_EPS) * gnw_ref[...] + gnb_ref[...]
    bonus = head_sums(r * kmod * rk_ref[...]) * v
    o_ref[...] = (on + bonus) * g


def _rwkv(z_rw, mu_p, w0, w2, a0, a2, g2, k_k, k_a, r_k, gn_w, gn_b):
    B, T, _ = z_rw.shape
    L = RW_CHUNK * RW_NCH
    row = lambda t: t.reshape(1, RWKV_W)
    w2p = jnp.concatenate([w2, jnp.zeros((LANES - D_DECAY_LORA, RWKV_W), w2.dtype)], axis=0).astype(bf16)
    a2p = jnp.concatenate([a2, jnp.zeros((LANES - D_AAA_LORA, RWKV_W), a2.dtype)], axis=0).astype(bf16)
    const = lambda shp: pl.BlockSpec(shp, lambda b, c: (0,) * len(shp))
    return pl.pallas_call(
        _rwkv_body,
        grid=(B, T // L),
        in_specs=[pl.BlockSpec((None, L, RW_P), lambda b, c: (b, c, 0)),
                  const((1, RW_P)), const((1, RWKV_W)), const((LANES, RWKV_W)), const((1, RWKV_W)),
                  const((LANES, RWKV_W)), const((D_GATE_LORA, RWKV_W)), const((1, RWKV_W)), const((1, RWKV_W)),
                  const((1, RWKV_W)), const((1, RWKV_W)), const((1, RWKV_W))],
        out_specs=pl.BlockSpec((None, L, RWKV_W), lambda b, c: (b, c, 0)),
        out_shape=jax.ShapeDtypeStruct((B, T, RWKV_W), f32),
        scratch_shapes=[pltpu.VMEM((RWKV_W // LANES, LANES, LANES), f32),
                        pltpu.VMEM((SUBLANES, RW_P), f32)],
        compiler_params=_cparams("parallel", "arbitrary"),
        name="rwkv7",
    )(z_rw, mu_p, row(w0), w2p, row(a0), a2p, g2.astype(bf16), row(k_k), row(k_a), row(r_k), row(gn_w), row(gn_b))


def _cmp_body(x2_ref, w1k_ref, w1v_ref, c0k_ref, c0v_ref, w2k_ref, w2v_ref, ko_ref, vto_ref):
    half = w1k_ref.shape[0] // 2
    x2 = x2_ref[...]
    nrow = x2.shape[0]

    def mlp(w1_ref, c0_ref, w2_ref):
        ha = jnp.dot(x2, w1_ref[0:half, :], preferred_element_type=f32)
        hb = jnp.dot(x2, w1_ref[half:2 * half, :], preferred_element_type=f32)
        hb_next = pltpu.roll(hb, nrow - 1, 0)
        out = _dot(_gelu_tanh(ha + hb_next + c0_ref[...]), w2_ref[...])
        n = lax.broadcasted_iota(jnp.int32, out.shape, 0)
        return jnp.where(n < nrow - 1, out, 0.0)

    k_cmp = mlp(w1k_ref, c0k_ref, w2k_ref)
    n1 = lax.broadcasted_iota(jnp.int32, (nrow, 1), 0)
    tail = _key_tail(n1 * CMP_STRIDE + (CMP_LEN - 1), False)
    ko_ref[...] = (jnp.concatenate([k_cmp, jnp.zeros((nrow, LANES - HEAD_DIM), f32)], axis=1) + tail).astype(bf16)
    vto_ref[...] = mlp(w1v_ref, c0v_ref, w2v_ref).T.astype(bf16)


def _compress(kcv, pe_k, pe_v, w1_k, w2_k, w1_v, w2_v):
    B, G, T, _ = kcv.shape
    M = T // CMP_STRIDE
    Dh = HEAD_DIM
    x2 = kcv.reshape(B, G, M, CMP_STRIDE * LANES)
    w3k = w1_k.reshape(CMP_LEN, Dh, CMP_HIDDEN)
    w3v = w1_v.reshape(CMP_LEN, Dh, CMP_HIDDEN)
    zeros = jnp.zeros_like(w3k)
    w1k_e = jnp.concatenate([w3k, zeros], axis=1).reshape(CMP_LEN * LANES, CMP_HIDDEN).astype(bf16)
    w1v_e = jnp.concatenate([zeros, w3v], axis=1).reshape(CMP_LEN * LANES, CMP_HIDDEN).astype(bf16)
    flat = CMP_LEN * Dh
    c0k = jnp.dot(pe_k.reshape(1, flat), w1_k, precision=HI)
    c0v = jnp.dot(pe_v.reshape(1, flat), w1_v, precision=HI)
    const = lambda shp: pl.BlockSpec(shp, lambda b, g: (0,) * len(shp))
    return pl.pallas_call(
        _cmp_body,
        grid=(B, G),
        in_specs=[pl.BlockSpec((None, None, M, CMP_STRIDE * LANES), lambda b, g: (b, g, 0, 0)),
                  const((CMP_LEN * LANES, CMP_HIDDEN)), const((CMP_LEN * LANES, CMP_HIDDEN)),
                  const((1, CMP_HIDDEN)), const((1, CMP_HIDDEN)),
                  const((CMP_HIDDEN, Dh)), const((CMP_HIDDEN, Dh))],
        out_specs=[pl.BlockSpec((None, None, M, LANES), lambda b, g: (b, g, 0, 0)),
                   pl.BlockSpec((None, None, Dh, M), lambda b, g: (b, g, 0, 0))],
        out_shape=[jax.ShapeDtypeStruct((B, G, M, LANES), bf16), jax.ShapeDtypeStruct((B, G, Dh, M), bf16)],
        compiler_params=_cparams("parallel", "parallel"),
        name="nsa_compress",
    )(x2, w1k_e, w1v_e, c0k, c0v, w2_k.astype(bf16), w2_v.astype(bf16))


def _nsa_body(q_ref, kc_ref, vct_ref, ks_ref, vst_ref, kw_ref, vwt_ref, gl_ref, ovt_ref,
              o_ref, score_ref, cnt_ref, q3_ref, selg_ref, s_sel, p_sel, acc_sel, s_win, p_win, acc_win):
    QT = NSA_QT
    KB = NSA_KB
    R = NSA_Q_PER_KV
    Dh = HEAD_DIM
    T = ks_ref.shape[0]
    NB = T // SEL_LEN
    NC = kc_ref.shape[0]
    qi = pl.program_id(2)
    t0 = qi * QT

    W = R * QT
    q3_ref[...] = jnp.concatenate([q_ref[r * LANES:(r + 1) * LANES, :] for r in range(R)], axis=1)
    slope_rows = q3_ref[Dh:Dh + NSA_AUG, :].astype(f32)[0:4]
    tq1 = t0 + lax.broadcasted_iota(jnp.int32, (1, QT), 1)
    tile3 = lambda t: jnp.concatenate([t] * R, axis=1)

    def col_reduce(x, op):
        parts = [x[i * SUBLANES:(i + 1) * SUBLANES] for i in range(x.shape[0] // SUBLANES)]
        while len(parts) > 1:
            nxt = [op(parts[2 * i], parts[2 * i + 1]) for i in range(len(parts) // 2)]
            parts = nxt + parts[2 * (len(parts) // 2):]
        return parts[0]

    colmax = lambda x: jnp.max(col_reduce(x, jnp.maximum), axis=0, keepdims=True)
    colsum = lambda x: jnp.sum(col_reduce(x, jnp.add), axis=0, keepdims=True)

    n_c = lax.broadcasted_iota(jnp.int32, (NC, 1), 0)
    vis_c = jnp.where(((n_c * CMP_STRIDE + (CMP_LEN - 1)) <= tq1) & (n_c < NC - 1), 0.0, NEG_INF)
    s_c = jnp.dot(kc_ref[...], q3_ref[...], preferred_element_type=f32) + tile3(vis_c)
    m_c = colmax(s_c)
    e_c = jnp.exp2(s_c - m_c)
    p_c = e_c * jnp.where(m_c > 0.5 * NEG_INF, 1.0 / colsum(e_c), 0.0)
    o_c = jnp.dot(vct_ref[...], p_c.astype(bf16), preferred_element_type=f32)
    p_sum = p_c[:, 0:QT]
    for r in range(1, R):
        p_sum = p_sum + p_c[:, r * QT:(r + 1) * QT]
    imp = _split_dot(p_sum, ovt_ref[...], 3, True)

    j = lax.broadcasted_iota(jnp.int32, (NB, 1), 0)
    cur = tq1 >> (SEL_LEN.bit_length() - 1)
    valid = (j * SEL_LEN) <= tq1
    forced = (j == 0) | (j == cur) | (j == cur - 1)
    score_ref[...] = jnp.where(valid, imp + FORCED_BONUS * forced.astype(f32), -jnp.inf)
    cnt_ref[...] = jnp.zeros_like(cnt_ref)
    nrb = NB // SUBLANES
    jsub = lax.broadcasted_iota(jnp.int32, (SUBLANES, 1), 0)
    last_rb = (t0 + QT - 1) // (SUBLANES * SEL_LEN)
    for ib in range(nrb):
        @pl.when(ib <= last_rb)
        def _(ib=ib):
            rows = [jnp.broadcast_to(score_ref[ib * SUBLANES + ii:ib * SUBLANES + ii + 1, :], (SUBLANES, QT))
                    for ii in range(SUBLANES)]
            for jb in range(nrb):
                @pl.when(jb <= last_rb)
                def _(jb=jb):
                    sj = score_ref[jb * SUBLANES:(jb + 1) * SUBLANES, :]
                    acc = cnt_ref[jb * SUBLANES:(jb + 1) * SUBLANES, :]
                    for ii in range(SUBLANES):
                        if ib < jb:
                            ahead = rows[ii] >= sj
                        elif ib > jb:
                            ahead = rows[ii] > sj
                        else:
                            ahead = (rows[ii] > sj) | ((rows[ii] == sj) & (ii < jsub))
                        acc = acc + jnp.where(ahead, 1.0, 0.0)
                    cnt_ref[jb * SUBLANES:(jb + 1) * SUBLANES, :] = acc
    selected = (cnt_ref[...] < float(min(SEL_TOPK, NB))) & valid
    sel_add = tile3(jnp.where(selected, 0.0, NEG_INF))
    spb = KB // SEL_LEN
    pad_rows = jnp.zeros((NSA_AUG - 4 - spb, W), f32)
    for tb in range(T // KB):
        selg_ref[tb] = jnp.concatenate([slope_rows, sel_add[tb * spb:(tb + 1) * spb], pad_rows], axis=0).astype(bf16)

    def stream(k_ref, vt_ref, n_plain, n_masked, start_of, mask_of, before_scores, s_buf, p_buf, acc_ref):
        n_all = n_plain + n_masked

        def scores(i, slot):
            ic = jnp.minimum(i, n_all - 1)
            before_scores(ic)
            s_buf[slot] = jnp.dot(k_ref[pl.ds(start_of(ic), KB), :], q3_ref[...], preferred_element_type=f32)

        def body(i, carry, masked, slot):
            m, l, alpha_prev = carry
            vblk = vt_ref[:, pl.ds(start_of(jnp.maximum(i - 1, 0)), KB)]
            pv = jnp.dot(vblk, p_buf[1 - slot], preferred_element_type=f32)
            scores(i + 1, 1 - slot)
            s = s_buf[slot]
            if masked:
                s = s + tile3(mask_of(i))
            m_new = jnp.maximum(m, colmax(s))
            alpha = jnp.exp2(m - m_new)
            p = jnp.exp2(s - m_new)
            p_buf[slot] = p.astype(bf16)
            acc_ref[...] = alpha_prev * acc_ref[...] + pv
            return m_new, alpha * l + colsum(p), alpha

        p_buf[1] = jnp.zeros((KB, W), bf16)
        acc_ref[...] = jnp.zeros_like(acc_ref)
        scores(0, 0)
        carry = (jnp.full((1, W), NEG_INF, f32), jnp.zeros((1, W), f32), jnp.ones((1, W), f32))

        def pair(j, carry, masked):
            return body(2 * j + 1, body(2 * j, carry, masked, 0), masked, 1)

        carry = lax.fori_loop(0, n_plain // 2, functools.partial(pair, masked=False), carry)
        _, l, alpha_last = lax.fori_loop(n_plain // 2, n_all // 2, functools.partial(pair, masked=True), carry)
        pv = jnp.dot(vt_ref[:, pl.ds(start_of(n_all - 1), KB)], p_buf[1], preferred_element_type=f32)
        return (alpha_last * acc_ref[...] + pv) * (1.0 / l)

    rel = lax.broadcasted_iota(jnp.int32, (1, QT), 1) - lax.broadcasted_iota(jnp.int32, (KB, 1), 0)

    nfull = qi * (QT // KB)

    def load_sel_rows(i):
        q3_ref[Dh:Dh + NSA_AUG, :] = selg_ref[i]

    o_s = stream(ks_ref, vst_ref, nfull, QT // KB,
                 lambda i: pl.multiple_of(i * KB, KB),
                 lambda i: jnp.where(rel - (i - nfull) * KB >= 0, 0.0, NEG_INF),
                 load_sel_rows, s_sel, p_sel, acc_sel)

    def win_mask(i):
        off = i * KB - WINDOW
        d = rel - off
        return jnp.where((d >= 0) & (d < WINDOW) & (t0 + off >= 0), 0.0, NEG_INF)

    o_w = stream(kw_ref, vwt_ref, 0, (WINDOW + QT) // KB,
                 lambda i: pl.multiple_of(jnp.maximum(t0 - WINDOW + i * KB, 0), KB),
                 win_mask, lambda i: None, s_win, p_win, acc_win)


    gates = _sigmoid(gl_ref[...])
    for r in range(R):
        ls = slice(r * QT, (r + 1) * QT)
        o_ref[r * Dh:(r + 1) * Dh, :] = (gates[3 * r:3 * r + 1, :] * o_c[:, ls]
                                          + gates[3 * r + 1:3 * r + 2, :] * o_s[:, ls]
                                          + gates[3 * r + 2:3 * r + 3, :] * o_w[:, ls]).astype(o_ref.dtype)


def _alibi_slopes(n):
    def pow2(m):
        start = 2.0 ** (-8.0 / m)
        return [start ** (i + 1) for i in range(m)]
    if math.log2(n).is_integer():
        return pow2(n)
    c = 2 ** math.floor(math.log2(n))
    return pow2(c) + pow2(2 * c)[0::2][: n - c]


def _nsa_attend(q_t, k_cmp, v_cmp_t, ks, vs_t, kw, vw_t, gl_t):
    B, G, T, _ = ks.shape
    Dh = HEAD_DIM
    QT = NSA_QT
    R = NSA_Q_PER_KV
    NB = T // SEL_LEN
    NC = k_cmp.shape[2]
    assert QT % NSA_KB == 0 and WINDOW % NSA_KB == 0 and T % QT == 0 and NSA_KB % SEL_LEN == 0
    n = np.arange(NC)
    jb = np.arange(NB)
    ov = ((n[:, None] * CMP_STRIDE < jb[None, :] * SEL_LEN + SEL_LEN)
          & (n[:, None] * CMP_STRIDE + CMP_LEN - 1 >= jb[None, :] * SEL_LEN) & (n[:, None] < NC - 1))
    ovt = jnp.asarray(ov.T.astype(np.float32))
    per_bg = lambda shp: pl.BlockSpec((None, None) + shp, lambda b, g, i: (b, g, 0, 0))
    const = lambda shp: pl.BlockSpec(shp, lambda b, g, i: (0, 0))
    return pl.pallas_call(
        _nsa_body,
        grid=(B, G, T // QT),
        in_specs=[pl.BlockSpec((None, R * LANES, QT), lambda b, g, i: (b, g, i)),
                  per_bg((NC, LANES)), per_bg((Dh, NC)),
                  per_bg((T, LANES)), per_bg((Dh, T)), per_bg((T, LANES)), per_bg((Dh, T)),
                  pl.BlockSpec((None, GL_ROWS, QT), lambda b, g, i: (b, g, i)),
                  const((NB, NC))],
        out_specs=pl.BlockSpec((None, R * Dh, QT), lambda b, g, i: (b, g, i)),
        out_shape=jax.ShapeDtypeStruct((B, G * R * Dh, T), bf16),
        scratch_shapes=[pltpu.VMEM((NB, QT), f32), pltpu.VMEM((NB, QT), f32), pltpu.VMEM((LANES, R * QT), bf16),
                        pltpu.VMEM((T // NSA_KB, NSA_AUG, R * QT), bf16),
                        pltpu.VMEM((2, NSA_KB, R * QT), f32), pltpu.VMEM((2, NSA_KB, R * QT), bf16),
                        pltpu.VMEM((Dh, R * QT), f32),
                        pltpu.VMEM((2, NSA_KB, R * QT), f32), pltpu.VMEM((2, NSA_KB, R * QT), bf16),
                        pltpu.VMEM((Dh, R * QT), f32)],
        compiler_params=_cparams("parallel", "parallel", "arbitrary"),
        name="nsa_attend",
    )(q_t, k_cmp, v_cmp_t, ks, vs_t, kw, vw_t, gl_t, ovt)


def _s5_body(u_ref, tg_ref, bend_ref, cpow_ref, apow_ref, y_ref):
    B, CH, nk, Cs = u_ref.shape
    rows = B * nk
    conv = None
    inc = None
    for ci in range(CH):
        a = u_ref[:, ci].reshape(rows, Cs).astype(bf16)
        c1 = jnp.dot(a, tg_ref[ci], preferred_element_type=f32)
        i1 = jnp.dot(a, bend_ref[ci], preferred_element_type=f32)
        conv = c1 if conv is None else conv + c1
        inc = i1 if inc is None else inc + i1
    kidx = lax.broadcasted_iota(jnp.int32, (rows, 1), 0) & (nk - 1)
    x = inc
    sh = 1
    lvl = 0
    while sh < nk:
        prev = jnp.where(kidx >= sh, pltpu.roll(x, sh, 0), 0.0)
        x = x + apow_ref[lvl, 0:1, :] * prev + apow_ref[lvl, 1:2, :] * pltpu.roll(prev, S5_STATE, 1)
        sh *= 2
        lvl += 1
    start = jnp.where(kidx >= 1, pltpu.roll(x, 1, 0), 0.0)
    y = conv + jnp.dot(start.astype(bf16), cpow_ref[...], preferred_element_type=f32)
    for co in range(CH):
        y_ref[:, co] = y[:, co * Cs:(co + 1) * Cs].reshape(B, nk, Cs)


def _s5_tables(lam_re, lam_im, log_dt, b_re, b_im, c_re, c_im):
    Cs = S5_CHUNK
    lam_re, lam_im = lam_re.astype(f32), lam_im.astype(f32)
    b_re, b_im, c_re, c_im = (t.astype(f32) for t in (b_re, b_im, c_re, c_im))
    dt = jnp.exp(log_dt.astype(f32))[:, None]
    mag = jnp.exp(lam_re * dt)
    ab_re, ab_im = mag * jnp.cos(lam_im * dt), mag * jnp.sin(lam_im * dt)
    den = lam_re * lam_re + lam_im * lam_im
    f_re = ((ab_re - 1.0) * lam_re + ab_im * lam_im) / den
    f_im = (ab_im * lam_re - (ab_re - 1.0) * lam_im) / den
    bb_re = f_re[..., None] * b_re - f_im[..., None] * b_im
    bb_im = f_re[..., None] * b_im + f_im[..., None] * b_re
    tau = jnp.arange(Cs + 1, dtype=f32)[None, None, :]
    pmag = jnp.exp(lam_re[..., None] * dt[..., None] * tau)
    pang = lam_im[..., None] * dt[..., None] * tau
    pw_re, pw_im = pmag * jnp.cos(pang), pmag * jnp.sin(pang)
    ein = functools.partial(jnp.einsum, precision=HI)
    ca_re = c_re[..., None] * pw_re[:, None] - c_im[..., None] * pw_im[:, None]
    ca_im = c_re[..., None] * pw_im[:, None] + c_im[..., None] * pw_re[:, None]
    kern = ein('gopt,gpi->gtoi', ca_re[..., :Cs], bb_re) - ein('gopt,gpi->gtoi', ca_im[..., :Cs], bb_im)
    s_idx = jnp.arange(Cs)
    lag_np = np.arange(Cs)[None, :] - np.arange(Cs)[:, None]
    place = jnp.asarray((lag_np[:, :, None] == np.arange(Cs)[None, None, :]).astype(np.float32))
    tg = ein('stl,gloi->gisot', place, kern)
    tg = tg.reshape(S5_GROUPS, S5_CH, Cs, S5_CH * Cs)
    rev = (Cs - 1) - s_idx
    pe_re, pe_im = pw_re[:, :, rev], pw_im[:, :, rev]
    be_re = pe_re[..., None] * bb_re[:, :, None, :] - pe_im[..., None] * bb_im[:, :, None, :]
    be_im = pe_re[..., None] * bb_im[:, :, None, :] + pe_im[..., None] * bb_re[:, :, None, :]
    bend = jnp.concatenate([jnp.transpose(be_re, (0, 3, 2, 1)), jnp.transpose(be_im, (0, 3, 2, 1))], axis=-1)
    cp_re = jnp.transpose(ca_re[..., 1:], (0, 2, 1, 3))
    cp_im = jnp.transpose(ca_im[..., 1:], (0, 2, 1, 3))
    cpow = jnp.concatenate([cp_re, -cp_im], axis=1).reshape(S5_GROUPS, 2 * S5_STATE, S5_CH * Cs)
    lv = (Cs * 2.0 ** jnp.arange(S5_SCAN_LEVELS, dtype=f32))[None, None, :]
    qmag = jnp.exp(lam_re[..., None] * dt[..., None] * lv)
    qang = lam_im[..., None] * dt[..., None] * lv
    q_re, q_im = jnp.transpose(qmag * jnp.cos(qang), (0, 2, 1)), jnp.transpose(qmag * jnp.sin(qang), (0, 2, 1))
    apow = jnp.stack([jnp.concatenate([q_re, q_re], axis=-1), jnp.concatenate([-q_im, q_im], axis=-1)], axis=2)
    return tg.astype(bf16), bend.astype(bf16), cpow.astype(bf16), apow


def _s5_scan(u_t, tables):
    B, _, T = u_t.shape
    Cs = S5_CHUNK
    nk = T // Cs
    assert nk % SUBLANES == 0 and nk & (nk - 1) == 0 and nk <= 2 ** S5_SCAN_LEVELS
    tg, bend, cpow, apow = tables
    u4 = u_t.reshape(B, S5_W, nk, Cs)
    W = S5_CH * Cs
    per_g = lambda shp: pl.BlockSpec((None,) + shp, lambda g: (g,) + (0,) * len(shp))
    grp = pl.BlockSpec((B, S5_CH, nk, Cs), lambda g: (0, g, 0, 0))
    y4 = pl.pallas_call(
        _s5_body,
        grid=(S5_GROUPS,),
        in_specs=[grp, per_g((S5_CH, Cs, W)), per_g((S5_CH, Cs, 2 * S5_STATE)), per_g((2 * S5_STATE, W)),
                  per_g((S5_SCAN_LEVELS, 2, 2 * S5_STATE))],
        out_specs=grp,
        out_shape=jax.ShapeDtypeStruct((B, S5_W, nk, Cs), f32),
        compiler_params=_cparams("parallel"),
        name="s5_conv",
    )(u4, tg, bend, cpow, apow)
    return y4.reshape(B, S5_W, T)


def _s5_post_body(y_ref, u_ref, d_ref, w_ref, o_ref):
    y = _gelu_tanh(y_ref[...] + d_ref[...] * u_ref[...])
    vg = _dot_tn(y, w_ref[...])
    o_ref[...] = vg[:, 0:S5_W] * _sigmoid(vg[:, S5_W:2 * S5_W])


def _s5_post(y_t, u_t, d_skip, w_glu):
    B, _, T = y_t.shape
    tm = min(S5P_TM, T)
    blk = pl.BlockSpec((None, S5_W, tm), lambda b, i: (b, 0, i))
    return pl.pallas_call(
        _s5_post_body,
        grid=(B, T // tm),
        in_specs=[blk, blk, pl.BlockSpec((S5_W, 1), lambda b, i: (0, 0)),
                  pl.BlockSpec((S5_W, 2 * S5_W), lambda b, i: (0, 0))],
        out_specs=pl.BlockSpec((None, tm, S5_W), lambda b, i: (b, i, 0)),
        out_shape=jax.ShapeDtypeStruct((B, T, S5_W), f32),
        compiler_params=_cparams("parallel", "parallel"),
        name="s5_glu",
    )(y_t, u_t, d_skip.reshape(S5_W, 1), w_glu.astype(bf16))


def _out_proj_body(h_ref, orw_ref, onsat_ref, os5_ref, w_ref, g_ref, o_ref):
    mix = (_dot(orw_ref[...], w_ref[0:RWKV_W, :])
           + _dot_tn(onsat_ref[...], w_ref[RWKV_W:RWKV_W + NSA_W, :])
           + _dot(os5_ref[...], w_ref[RWKV_W + NSA_W:D_MIX, :]))
    o_ref[...] = h_ref[...] + _rms(mix, g_ref[...])


def _out_proj(h, o_rw, o_nsa_t, o_s5, w_out, g):
    B, T, _ = h.shape
    tm = min(OUT_TM, T)
    rows = lambda w: pl.BlockSpec((None, tm, w), lambda b, i: (b, i, 0))
    return pl.pallas_call(
        _out_proj_body,
        grid=(B, T // tm),
        in_specs=[rows(D_MODEL), rows(RWKV_W), pl.BlockSpec((None, NSA_W, tm), lambda b, i: (b, 0, i)), rows(S5_W),
                  pl.BlockSpec((D_MIX, D_MODEL), lambda b, i: (0, 0)),
                  pl.BlockSpec((1, D_MODEL), lambda b, i: (0, 0))],
        out_specs=rows(D_MODEL),
        out_shape=jax.ShapeDtypeStruct((B, T, D_MODEL), f32),
        compiler_params=_cparams("parallel", "parallel"),
        name="out_proj",
    )(h, o_rw, o_nsa_t, o_s5, w_out.astype(bf16), g.reshape(1, D_MODEL))


def _ffn_body(h_ref, hp_ref, gpre_ref, wg_ref, wu_ref, cwg_ref, cwu_ref, cbg_ref, cbu_ref, wd_ref, gpost_ref,
              o_ref, xn_ref, acc_ref, *, tiles_per_seq):
    tm = h_ref.shape[0]
    H = SUBLANES
    c = pl.program_id(1)

    @pl.when(c == 0)
    def _():
        first = (pl.program_id(0) % tiles_per_seq) == 0
        xp = _rms(hp_ref[...], gpre_ref[...])
        xn_ref[0:H, :] = jnp.where(first, 0.0, xp).astype(bf16)
        xn_ref[H:H + tm, :] = _rms(h_ref[...], gpre_ref[...]).astype(bf16)
        acc_ref[...] = jnp.zeros_like(acc_ref)

    xn = xn_ref[...]

    def conv_branch(w_ref, cw_ref, cb_ref):
        hu = jnp.dot(xn, w_ref[...], preferred_element_type=f32)
        cw = cw_ref[...]
        out = (cw[0:1, :] * pltpu.roll(hu, 2, 0)[H:H + tm]
               + cw[1:2, :] * pltpu.roll(hu, 1, 0)[H:H + tm]
               + cw[2:3, :] * hu[H:H + tm])
        return out + cb_ref[...]

    gate = conv_branch(wg_ref, cwg_ref, cbg_ref)
    up = conv_branch(wu_ref, cwu_ref, cbu_ref)
    acc_ref[...] += _dot(_gelu_tanh(gate) * up, wd_ref[...])

    @pl.when(c == pl.num_programs(1) - 1)
    def _():
        o_ref[...] = h_ref[...] + _rms(acc_ref[...], gpost_ref[...])


def _conv_ffn(h2d, seq_len, g_pre, w_up, conv_w, conv_b, w_down, g_post):
    n = h2d.shape[0]
    tm = min(FFN_TM, seq_len)
    tc = FFN_TC
    nc = D_FF // tc
    H = SUBLANES
    tiles_per_seq = seq_len // tm
    hpb = tm // H
    w_up = w_up.astype(bf16)
    cb = conv_b.reshape(1, 2 * D_FF)
    return pl.pallas_call(
        functools.partial(_ffn_body, tiles_per_seq=tiles_per_seq),
        grid=(n // tm, nc),
        in_specs=[pl.BlockSpec((tm, D_MODEL), lambda i, c: (i, 0)),
                  pl.BlockSpec((H, D_MODEL), lambda i, c: (jnp.maximum(i * hpb - 1, 0), 0)),
                  pl.BlockSpec((1, D_MODEL), lambda i, c: (0, 0)),
                  pl.BlockSpec((D_MODEL, tc), lambda i, c: (0, c)),
                  pl.BlockSpec((D_MODEL, tc), lambda i, c: (0, nc + c)),
                  pl.BlockSpec((CONV_W, tc), lambda i, c: (0, c)),
                  pl.BlockSpec((CONV_W, tc), lambda i, c: (0, nc + c)),
                  pl.BlockSpec((1, tc), lambda i, c: (0, c)),
                  pl.BlockSpec((1, tc), lambda i, c: (0, nc + c)),
                  pl.BlockSpec((tc, D_MODEL), lambda i, c: (c, 0)),
                  pl.BlockSpec((1, D_MODEL), lambda i, c: (0, 0))],
        out_specs=pl.BlockSpec((tm, D_MODEL), lambda i, c: (i, 0)),
        out_shape=jax.ShapeDtypeStruct((n, D_MODEL), f32),
        scratch_shapes=[pltpu.VMEM((H + tm, D_MODEL), bf16), pltpu.VMEM((tm, D_MODEL), f32)],
        compiler_params=_cparams("parallel", "arbitrary"),
        name="conv_ffn",
    )(h2d, h2d, g_pre.reshape(1, D_MODEL), w_up, w_up, conv_w, conv_w, cb, cb, w_down.astype(bf16),
      g_post.reshape(1, D_MODEL))


def _ple_body(h_ref, p_ref, wp_ref, gp_ref, wg_ref, o_ref):
    h = h_ref[...]
    e = _rms(_dot(p_ref[...], wp_ref[...]), gp_ref[...])
    gate = _sigmoid(_dot(h, wg_ref[...]))
    o_ref[...] = h + gate * e


def _ple(h2d, p2d, w_ple, g_ple, w_gate):
    n = h2d.shape[0]
    tm = min(PLE_TM, n)
    return pl.pallas_call(
        _ple_body,
        grid=(n // tm,),
        in_specs=[pl.BlockSpec((tm, D_MODEL), lambda i: (i, 0)),
                  pl.BlockSpec((tm, PLE_DIM), lambda i: (i, 0)),
                  pl.BlockSpec((PLE_DIM, D_MODEL), lambda i: (0, 0)),
                  pl.BlockSpec((1, D_MODEL), lambda i: (0, 0)),
                  pl.BlockSpec((D_MODEL, D_MODEL), lambda i: (0, 0))],
        out_specs=pl.BlockSpec((tm, D_MODEL), lambda i: (i, 0)),
        out_shape=jax.ShapeDtypeStruct((n, D_MODEL), f32),
        compiler_params=_cparams("parallel"),
        name="ple_gate",
    )(h2d, p2d, w_ple.astype(bf16), g_ple.reshape(1, D_MODEL), w_gate.astype(bf16))


def kernel(x, p, pre_mix_norm, post_mix_norm, pre_ffn_norm, post_ffn_norm, w_in, w_out, shift_mu, rw_w0, rw_w2, rw_a0, rw_a2, rw_g2, rw_k_k, rw_k_a, rw_r_k, rw_gn_w, rw_gn_b, cmp_pe_k, cmp_pe_v, cmp_w1_k, cmp_w2_k, cmp_w1_v, cmp_w2_v, s5_lam_re, s5_lam_im, s5_log_dt, s5_b_re, s5_b_im, s5_c_re, s5_c_im, s5_d, s5_w_glu, w_up, conv_w, conv_b, w_down, w_ple, ple_norm, w_ple_gate):
    B, T, D = x.shape
    n = B * T
    depth = w_in.shape[0]
    h = x
    for i in range(depth):
        wn, wt, qaug, mu_p = _prep_in_weights(w_in[i], shift_mu[i])
        z_rw, z_s5, ks, kw, kcv, q_t, vs_t, vw_t, gl_t = _in_proj(h, pre_mix_norm[i], wn, wt, qaug)
        o_rw = _rwkv(z_rw, mu_p, rw_w0[i], rw_w2[i], rw_a0[i], rw_a2[i], rw_g2[i],
                     rw_k_k[i], rw_k_a[i], rw_r_k[i], rw_gn_w[i], rw_gn_b[i])
        k_cmp, v_cmp_t = _compress(kcv, cmp_pe_k[i], cmp_pe_v[i], cmp_w1_k[i], cmp_w2_k[i], cmp_w1_v[i], cmp_w2_v[i])
        o_nsa_t = _nsa_attend(q_t, k_cmp, v_cmp_t, ks, vs_t, kw, vw_t, gl_t)
        tables = _s5_tables(s5_lam_re[i], s5_lam_im[i], s5_log_dt[i], s5_b_re[i], s5_b_im[i], s5_c_re[i], s5_c_im[i])
        o_s5 = _s5_post(_s5_scan(z_s5, tables), z_s5, s5_d[i], s5_w_glu[i])
        h = _out_proj(h, o_rw, o_nsa_t, o_s5, w_out[i], post_mix_norm[i])
        h = _conv_ffn(h.reshape(n, D), T, pre_ffn_norm[i], w_up[i], conv_w[i], conv_b[i], w_down[i], post_ffn_norm[i])
        h = _ple(h, p[i].reshape(n, PLE_DIM), w_ple[i], ple_norm[i], w_ple_gate[i]).reshape(B, T, D)
    return h
```

```python
import functools
import math

import numpy as np
import jax
import jax.numpy as jnp
from jax import lax
from jax.experimental import pallas as pl
from jax.experimental.pallas import tpu as pltpu

f32 = jnp.float32
bf16 = jnp.bfloat16
HI = lax.Precision.HIGHEST

D_MODEL = 1024
HEAD_DIM = 64
RWKV_HEADS = 6
RWKV_W = RWKV_HEADS * HEAD_DIM
D_DECAY_LORA = 64
D_AAA_LORA = 64
D_GATE_LORA = 128
RWKV_COLS = 3 * RWKV_W + D_DECAY_LORA + D_AAA_LORA + D_GATE_LORA
GN_EPS = 64e-5
NSA_Q_HEADS = 6
NSA_KV_GROUPS = 2
NSA_Q_PER_KV = NSA_Q_HEADS // NSA_KV_GROUPS
NSA_W = NSA_Q_HEADS * HEAD_DIM
NSA_KV_W = NSA_KV_GROUPS * HEAD_DIM
NSA_N_BRANCH = 3
NSA_COLS = NSA_W + 6 * NSA_KV_W + NSA_Q_HEADS * NSA_N_BRANCH
CMP_LEN = 32
CMP_STRIDE = 16
CMP_HIDDEN = 128
SEL_LEN = 64
SEL_TOPK = 16
WINDOW = 512
FORCED_BONUS = 1e3
NEG_INF = -1e30
S5_GROUPS = 16
S5_CH = 16
S5_W = S5_GROUPS * S5_CH
S5_STATE = 64
D_MIX = RWKV_W + NSA_W + S5_W
D_FF = 2816
CONV_W = 3
PLE_DIM = 256
NORM_EPS = 1e-6

LANES = 128
SUBLANES = 8
VMEM_LIMIT = 56 * 1024 * 1024

RW_LORA_W_OFF = 3 * RWKV_W
RW_LORA_A_OFF = RW_LORA_W_OFF + LANES
RW_LORA_G_OFF = RW_LORA_A_OFF + LANES
RW_P = RW_LORA_G_OFF + D_GATE_LORA
NSA_GATE_OFF = NSA_W + 6 * NSA_KV_W
NAT_KSW = RW_P
NAT_KCV = NAT_KSW + NSA_KV_GROUPS * 2 * LANES
NAT_P = NAT_KCV + NSA_KV_GROUPS * LANES
GL_ROWS = 16
TR_VS = NSA_Q_HEADS * LANES
TR_VW = TR_VS + NSA_KV_W
TR_GL = TR_VW + NSA_KV_W
TR_S5 = TR_GL + NSA_KV_GROUPS * GL_ROWS
TR_P = TR_S5 + S5_W

IN_TM = 256
RW_CHUNK = 64
RW_NCH = 4
NSA_QT = 512
NSA_KB = 256
NSA_AUG = 16
S5_CHUNK = 64
S5_SCAN_LEVELS = 8
OUT_TM = 512
FFN_TM = 512
FFN_TC = 1408
PLE_TM = 512
S5P_TM = 1024


def _cparams(*sem):
    return pltpu.CompilerParams(dimension_semantics=sem, vmem_limit_bytes=VMEM_LIMIT)


def _rms(x, g):
    ms = jnp.mean(x * x, axis=-1, keepdims=True)
    return x * lax.rsqrt(ms + NORM_EPS) * g


def _gelu_tanh(x):
    return 0.5 * x * (1.0 + jnp.tanh(math.sqrt(2.0 / math.pi) * (x + 0.044715 * (x * x * x))))


def _sigmoid(x):
    return 1.0 / (1.0 + jnp.exp(-x))


def _softplus(x):
    return jnp.maximum(x, 0.0) + jnp.log(1.0 + jnp.exp(-jnp.abs(x)))


def _dot(a, b):
    return jnp.dot(a.astype(bf16), b.astype(bf16), preferred_element_type=f32)


def _dot_nt(a, b):
    return lax.dot_general(a.astype(bf16), b.astype(bf16), (((1,), (1,)), ((), ())), preferred_element_type=f32)


def _dot_tn(a, b):
    return lax.dot_general(a.astype(bf16), b.astype(bf16), (((0,), (0,)), ((), ())), preferred_element_type=f32)


def _dot_hi(a, b):
    return jnp.dot(a, b, preferred_element_type=f32, precision=HI)


def _split_dot(x, m01, terms, m_left):
    m = m01.astype(bf16)
    acc = None
    rem = x
    for _ in range(terms):
        piece = rem.astype(bf16)
        d = (jnp.dot(m, piece, preferred_element_type=f32) if m_left
             else jnp.dot(piece, m, preferred_element_type=f32))
        acc = d if acc is None else acc + d
        rem = rem - piece.astype(f32)
    return acc


def _key_tail(t, sel_onehot):
    col = lax.broadcasted_iota(jnp.int32, t.shape[:1] + (LANES,), 1) - HEAD_DIM
    lo = (t & (LANES - 1)).astype(f32)
    hi = (t >> (LANES.bit_length() - 1)).astype(f32)
    tail = jnp.where((col == 0) | (col == 2), lo, jnp.where((col == 1) | (col == 3), hi, 0.0))
    if sel_onehot:
        spb = NSA_KB // SEL_LEN
        blk = (t >> (SEL_LEN.bit_length() - 1)) & (spb - 1)
        tail = tail + jnp.where(col - 4 == blk, 1.0, 0.0) * jnp.where((col >= 4) & (col < 4 + spb), 1.0, 0.0)
    return tail


def _in_proj_body(x_ref, g_ref, wn_ref, wt_ref, qaug_ref,
                  zrw_ref, zs5_ref, ks_ref, kw_ref, kcv_ref, qt_ref, vst_ref, vwt_ref, glt_ref, kcv_scr):
    tm = x_ref.shape[0]
    G, Dh = NSA_KV_GROUPS, HEAD_DIM
    xn = _rms(x_ref[...], g_ref[...]).astype(bf16)
    zrw_ref[...] = jnp.dot(xn, wn_ref[:, 0:RW_P], preferred_element_type=f32)
    ksw =jnp.dot(xn, wn_ref[:, NAT_KSW:NAT_KCV], preferred_element_type=f32)
    tok = pl.program_id(1) * tm + lax.broadcasted_iota(jnp.int32, (tm, 1), 0)
    tail_w = _key_tail(tok, False)
    tail_s = _key_tail(tok, True)
    for g in range(G):
        ks_ref[g] = (ksw[:, 2 * g * LANES:(2 * g + 1) * LANES] + tail_s).astype(bf16)
        kw_ref[g] = (ksw[:, (2 * g + 1) * LANES:(2 * g + 2) * LANES] + tail_w).astype(bf16)
    kcv = jnp.dot(xn, wn_ref[:, NAT_KCV:NAT_P], preferred_element_type=f32)
    for g in range(G):
        kcv_scr[...] = kcv[:, g * LANES:(g + 1) * LANES]
        kcv_ref[g] = jnp.concatenate([kcv_scr[pl.ds(l, tm // CMP_STRIDE, stride=CMP_STRIDE), :]
                                      for l in range(CMP_STRIDE)], axis=1).astype(bf16)
    zt = lax.dot_general(wt_ref[...], xn, (((1,), (1,)), ((), ())), preferred_element_type=f32)
    qscale = HEAD_DIM ** -0.5 * math.log2(math.e)
    qt_ref[...] = (zt[0:TR_VS] * qscale + qaug_ref[...]).astype(bf16)
    for g in range(G):
        vst_ref[g] = zt[TR_VS + g * Dh:TR_VS + (g + 1) * Dh].astype(bf16)
        vwt_ref[g] = zt[TR_VW + g * Dh:TR_VW + (g + 1) * Dh].astype(bf16)
    glt_ref[...] = zt[TR_GL:TR_S5]
    zs5_ref[...] = zt[TR_S5:TR_P]


def _in_proj(h, g, wn, wt, qaug):
    B, T, _ = h.shape
    tm = min(IN_TM, T)
    G, Dh = NSA_KV_GROUPS, HEAD_DIM
    const = lambda shp: pl.BlockSpec(shp, lambda b, i: (0,) * len(shp))
    tok3 = lambda w: pl.BlockSpec((None, tm, w), lambda b, i: (b, i, 0))
    tok4 = lambda w: pl.BlockSpec((None, G, tm, w), lambda b, i: (b, 0, i, 0))
    trn3 = lambda r: pl.BlockSpec((None, r, tm), lambda b, i: (b, 0, i))
    trn4 = pl.BlockSpec((None, G, Dh, tm), lambda b, i: (b, 0, 0, i))
    S = jax.ShapeDtypeStruct
    return pl.pallas_call(
        _in_proj_body,
        grid=(B, T // tm),
        in_specs=[tok3(D_MODEL), const((1, D_MODEL)), const((D_MODEL, NAT_P)), const((TR_P, D_MODEL)),
                  const((TR_VS, 1))],
        out_specs=[tok3(RW_P), trn3(S5_W), tok4(LANES), tok4(LANES),
                   pl.BlockSpec((None, G, tm // CMP_STRIDE, CMP_STRIDE * LANES), lambda b, i: (b, 0, i, 0)),
                   trn3(TR_VS), trn4, trn4, trn3(TR_S5 - TR_GL)],
        out_shape=[S((B, T, RW_P), f32), S((B, S5_W, T), f32), S((B, G, T, LANES), bf16), S((B, G, T, LANES), bf16),
                   S((B, G, T // CMP_STRIDE, CMP_STRIDE * LANES), bf16), S((B, TR_VS, T), bf16),
                   S((B, G, Dh, T), bf16), S((B, G, Dh, T), bf16), S((B, TR_S5 - TR_GL, T), f32)],
        scratch_shapes=[pltpu.VMEM((tm, LANES), f32)],
        compiler_params=_cparams("parallel", "parallel"),
        name="in_proj",
    )(h, g.reshape(1, D_MODEL), wn, wt, qaug)


def _prep_in_weights(w_in, shift_mu):
    d = w_in.shape[0]
    G, R, Dh = NSA_KV_GROUPS, NSA_Q_PER_KV, HEAD_DIM
    zc = lambda k: jnp.zeros((d, k), w_in.dtype)
    o = 3 * RWKV_W
    rw = [w_in[:, :o],
          w_in[:, o:o + D_DECAY_LORA], zc(LANES - D_DECAY_LORA),
          w_in[:, o + D_DECAY_LORA:o + D_DECAY_LORA + D_AAA_LORA], zc(LANES - D_AAA_LORA),
          w_in[:, o + D_DECAY_LORA + D_AAA_LORA:RWKV_COLS]]
    nsa = w_in[:, RWKV_COLS:RWKV_COLS + NSA_COLS]
    s5 = w_in[:, RWKV_COLS + NSA_COLS:]
    sec = lambda i, g: nsa[:, NSA_W + i * NSA_KV_W + g * Dh:NSA_W + i * NSA_KV_W + (g + 1) * Dh]
    ksw = [t for g in range(G) for t in (sec(2, g), zc(LANES - Dh), sec(4, g), zc(LANES - Dh))]
    kcv = [t for g in range(G) for t in (sec(0, g), sec(1, g))]
    wn = jnp.concatenate(rw + ksw + kcv, axis=1).astype(bf16)
    q_cols = [t for h in range(G * R) for t in (nsa[:, h * Dh:(h + 1) * Dh], zc(LANES - Dh))]
    nb = R * NSA_N_BRANCH
    gl_cols = [t for g in range(G) for t in (nsa[:, NSA_GATE_OFF + g * nb:NSA_GATE_OFF + (g + 1) * nb],
                                             zc(GL_ROWS - nb))]
    wt = jnp.concatenate(q_cols + [sec(3, g) for g in range(G)] + [sec(5, g) for g in range(G)] + gl_cols + [s5],
                         axis=1)
    wt = wt.T.astype(bf16)
    sl = np.asarray(_alibi_slopes(NSA_Q_HEADS), np.float64) * math.log2(math.e)
    sl_hi = sl.astype(bf16).astype(np.float64)
    sl_lo = sl - sl_hi
    qaug = np.zeros((G * R, LANES), np.float32)
    qaug[:, Dh:Dh + 4] = np.stack([sl_hi, sl_hi * LANES, sl_lo, sl_lo * LANES], axis=-1)
    z1 = lambda k: jnp.zeros((k,), shift_mu.dtype)
    mu_p = jnp.concatenate([shift_mu[:o],
                            shift_mu[o:o + D_DECAY_LORA], z1(LANES - D_DECAY_LORA),
                            shift_mu[o + D_DECAY_LORA:o + D_DECAY_LORA + D_AAA_LORA], z1(LANES - D_AAA_LORA),
                            shift_mu[o + D_DECAY_LORA + D_AAA_LORA:]])
    return wn, wt, jnp.asarray(qaug.reshape(TR_VS, 1)), mu_p.reshape(1, RW_P)


def _rwkv_body(z_ref, mu_ref, w0_ref, w2_ref, a0_ref, a2_ref, g2_ref, kk_ref, ka_ref, rk_ref, gnw_ref, gnb_ref,
               o_ref, state_ref, prev_ref):
    C = RW_CHUNK
    N = HEAD_DIM
    L = z_ref.shape[0]
    nch = L // C
    npair = RWKV_W // LANES

    @pl.when(pl.program_id(1) == 0)
    def _():
        state_ref[...] = jnp.zeros_like(state_ref)
        prev_ref[...] = jnp.zeros_like(prev_ref)

    z = z_ref[...]
    row = lax.broadcasted_iota(jnp.int32, z.shape, 0)
    zprev = jnp.where(row == 0, prev_ref[0:1, :], pltpu.roll(z, 1, 0))
    prev_ref[0:1, :] = z[L - 1:L, :]
    zs = z + (zprev - z) * mu_ref[...]

    r = zs[:, 0:RWKV_W]
    k = zs[:, RWKV_W:2 * RWKV_W]
    v = zs[:, 2 * RWKV_W:3 * RWKV_W]
    wl = zs[:, RW_LORA_W_OFF:RW_LORA_W_OFF + LANES]
    al = zs[:, RW_LORA_A_OFF:RW_LORA_A_OFF + LANES]
    gl = zs[:, RW_LORA_G_OFF:RW_LORA_G_OFF + D_GATE_LORA]
    lane = lax.broadcasted_iota(jnp.int32, (1, LANES), 1)
    li = lax.broadcasted_iota(jnp.int32, (LANES, LANES), 0)
    lj = lax.broadcasted_iota(jnp.int32, (LANES, LANES), 1)
    hshift = N.bit_length() - 1
    bd128 = ((li >> hshift) == (lj >> hshift)).astype(f32)
    eye128 = (li == lj).astype(f32)
    first_half = lane < N
    lane3 = lax.broadcasted_iota(jnp.int32, (1, RWKV_W), 1)
    half_mask = [(((lane3 >> hshift) & 1) == hh).astype(f32) for hh in range(2)]

    def head_sums(x):
        return jnp.concatenate([_split_dot(x[:, j * LANES:(j + 1) * LANES], bd128, 2, False)
                                for j in range(npair)], axis=1)

    w = -_softplus(-(w0_ref[...] + _dot(jnp.tanh(wl), w2_ref[...]))) - 0.5
    logd = -jnp.exp(w)
    a = _sigmoid(a0_ref[...] + _dot(al, a2_ref[...]))
    g = _dot(_sigmoid(gl), g2_ref[...])
    kkr = k * kk_ref[...]
    kk = kkr / jnp.maximum(jnp.sqrt(head_sums(kkr * kkr)), 1e-12)
    kmod = k * (1.0 + (a - 1.0) * ka_ref[...])
    b = kk * a

    ti = lax.broadcasted_iota(jnp.int32, (C, C), 0)
    si = lax.broadcasted_iota(jnp.int32, (C, C), 1)
    tri_incl = (si <= ti).astype(f32)
    tri_strict = (si < ti).astype(f32)
    eye = (si == ti).astype(f32)
    cums, tots = [], []
    for c in range(nch):
        cc = _split_dot(logd[c * C:(c + 1) * C], tri_incl, 3, True)
        cums.append(cc)
        tots.append(jnp.broadcast_to(cc[C - 1:C], (C, RWKV_W)))
    cum = jnp.concatenate(cums, axis=0)
    tot = jnp.concatenate(tots, axis=0)
    e_inv = jnp.exp(-cum)
    e_end = jnp.exp(tot - cum)
    r_t = r * jnp.exp(cum)
    kk_t = kk * jnp.exp(cum - logd)
    k_h = kmod * e_inv
    b_h = b * e_inv
    k_e = kmod * e_end
    b_e = b * e_end
    e_tot = jnp.exp(tot)
    kk_t_m = [kk_t * m for m in half_mask]
    r_t_m = [r_t * m for m in half_mask]
    k_e_m = [k_e * m for m in half_mask]
    b_e_m = [b_e * m for m in half_mask]

    lvl_masks = []
    s = 1
    while s < C:
        sh = s.bit_length() - 1
        m = ((ti >> (sh + 1)) == (si >> (sh + 1))) & (((ti >> sh) & 1) == 1) & (((si >> sh) & 1) == 0)
        lvl_masks.append(m.astype(f32))
        s *= 2

    chains = [(c, h) for c in range(nch) for h in range(RWKV_HEADS)]

    def blk(arr, c, h):
        j = h // 2
        return arr[c * C:(c + 1) * C, j * LANES:(j + 1) * LANES]

    p = [_dot_nt(jnp.concatenate([blk(kk_t_m[h % 2], c, h), blk(r_t_m[h % 2], c, h)], axis=0),
                 jnp.concatenate([blk(b_h, c, h), blk(k_h, c, h)], axis=0)) for c, h in chains]
    a_ab = [t[0:C, 0:C] * tri_strict for t in p]
    a_ak = [t[0:C, C:2 * C] * tri_strict for t in p]
    a_rb = [t[C:2 * C, 0:C] * tri_incl for t in p]
    a_rk = [t[C:2 * C, C:2 * C] * tri_incl for t in p]
    x = [eye - t * lvl_masks[0] for t in a_ab]
    for m in lvl_masks[1:]:
        t1 = [_dot(xg, ag * m) for xg, ag in zip(x, a_ab)]
        x = [xg - _dot(tg, xg) for xg, tg in zip(x, t1)]
    av = [_dot(jnp.concatenate([a_ak[i], a_rk[i]], axis=0), blk(v, c, h)) for i, (c, h) in enumerate(chains)]
    wu = [_dot(x[i], jnp.concatenate([blk(kk_t, c, h), av[i][0:C]], axis=1)) for i, (c, h) in enumerate(chains)]
    bwu = [_dot_tn(blk(b_e_m[h % 2], c, h), wu[i]) for i, (c, h) in enumerate(chains)]
    ktv = [_dot_tn(blk(k_e_m[h % 2], c, h), blk(v, c, h)) for c, h in chains]
    arb = [_dot(a_rb[i], wu[i]) for i in range(len(chains))]

    state = [state_ref[j] for j in range(npair)]
    out_rows = []
    for c in range(nch):
        out_lanes = []
        for j in range(npair):
            ge = c * RWKV_HEADS + 2 * j
            go = ge + 1
            bsum = bwu[ge] + bwu[go]
            tm = eye128 * e_tot[c * C:c * C + 1, j * LANES:(j + 1) * LANES] - bsum[:, 0:LANES] * bd128
            zc = (ktv[ge] + ktv[go] - bsum[:, LANES:2 * LANES]) * bd128
            q = blk(r_t, c, 2 * j) - jnp.where(first_half, arb[ge][:, 0:LANES], arb[go][:, 0:LANES])
            o_in = jnp.where(first_half, av[ge][C:2 * C] - arb[ge][:, LANES:2 * LANES],
                             av[go][C:2 * C] - arb[go][:, LANES:2 * LANES])
            ts = _dot(jnp.concatenate([tm, q], axis=0), state[j])
            state[j] = ts[0:LANES] + zc
            out_lanes.append(ts[LANES:LANES + C] + o_in)
        out_rows.append(jnp.concatenate(out_lanes, axis=1))
    for j in range(npair):
        state_ref[j] = state[j]
    o = jnp.concatenate(out_rows, axis=0)

    inv_n = 1.0 / N
    mean = head_sums(o) * inv_n
    oc = o - mean
    var = head_sums(oc * oc) * inv_n
    on = oc * lax.rsqrt(var + GN_EPS) * gnw_ref[...] + gnb_ref[...]
    bonus = head_sums(r * kmod * rk_ref[...]) * v
    o_ref[...] = (on + bonus) * g


def _rwkv(z_rw, mu_p, w0, w2, a0, a2, g2, k_k, k_a, r_k, gn_w, gn_b):
    B, T, _ = z_rw.shape
    L = RW_CHUNK * RW_NCH
    row = lambda t: t.reshape(1, RWKV_W)
    w2p = jnp.concatenate([w2, jnp.zeros((LANES - D_DECAY_LORA, RWKV_W), w2.dtype)], axis=0).astype(bf16)
    a2p = jnp.concatenate([a2, jnp.zeros((LANES - D_AAA_LORA, RWKV_W), a2.dtype)], axis=0).astype(bf16)
    const = lambda shp: pl.BlockSpec(shp, lambda b, c: (0,) * len(shp))
    return pl.pallas_call(
        _rwkv_body,
        grid=(B, T // L),
        in_specs=[pl.BlockSpec((None, L, RW_P), lambda b, c: (b, c, 0)),
                  const((1, RW_P)), const((1, RWKV_W)), const((LANES, RWKV_W)), const((1, RWKV_W)),
                  const((LANES, RWKV_W)), const((D_GATE_LORA, RWKV_W)), const((1, RWKV_W)), const((1, RWKV_W)),
                  const((1, RWKV_W)), const((1, RWKV_W)), const((1, RWKV_W))],
        out_specs=pl.BlockSpec((None, L, RWKV_W), lambda b, c: (b, c, 0)),
        out_shape=jax.ShapeDtypeStruct((B, T, RWKV_W), f32),
        scratch_shapes=[pltpu.VMEM((RWKV_W // LANES, LANES, LANES), f32),
                        pltpu.VMEM((SUBLANES, RW_P), f32)],
        compiler_params=_cparams("parallel", "arbitrary"),
        name="rwkv7",
    )(z_rw, mu_p, row(w0), w2p, row(a0), a2p, g2.astype(bf16), row(k_k), row(k_a), row(r_k), row(gn_w), row(gn_b))


def _cmp_body(x2_ref, w1k_ref, w1v_ref, c0k_ref, c0v_ref, w2k_ref, w2v_ref, ko_ref, vto_ref):
    half = w1k_ref.shape[0] // 2
    x2 = x2_ref[...]
    nrow = x2.shape[0]

    def mlp(w1_ref, c0_ref, w2_ref):
        ha = jnp.dot(x2, w1_ref[0:half, :], preferred_element_type=f32)
        hb = jnp.dot(x2, w1_ref[half:2 * half, :], preferred_element_type=f32)
        hb_next = pltpu.roll(hb, nrow - 1, 0)
        out = _dot(_gelu_tanh(ha + hb_next + c0_ref[...]), w2_ref[...])
        n = lax.broadcasted_iota(jnp.int32, out.shape, 0)
        return jnp.where(n < nrow - 1, out, 0.0)

    k_cmp = mlp(w1k_ref, c0k_ref, w2k_ref)
    n1 = lax.broadcasted_iota(jnp.int32, (nrow, 1), 0)
    tail = _key_tail(n1 * CMP_STRIDE + (CMP_LEN - 1), False)
    ko_ref[...] = (jnp.concatenate([k_cmp, jnp.zeros((nrow, LANES - HEAD_DIM), f32)], axis=1) + tail).astype(bf16)
    vto_ref[...] = mlp(w1v_ref, c0v_ref, w2v_ref).T.astype(bf16)


def _compress(x2, pe_k, pe_v, w1_k, w2_k, w1_v, w2_v):
    B, G, M, _ = x2.shape
    Dh = HEAD_DIM
    w3k = w1_k.reshape(CMP_LEN, Dh, CMP_HIDDEN)
    w3v = w1_v.reshape(CMP_LEN, Dh, CMP_HIDDEN)
    zeros = jnp.zeros_like(w3k)
    w1k_e = jnp.concatenate([w3k, zeros], axis=1).reshape(CMP_LEN * LANES, CMP_HIDDEN).astype(bf16)
    w1v_e = jnp.concatenate([zeros, w3v], axis=1).reshape(CMP_LEN * LANES, CMP_HIDDEN).astype(bf16)
    flat = CMP_LEN * Dh
    c0k = jnp.dot(pe_k.reshape(1, flat), w1_k, precision=HI)
    c0v = jnp.dot(pe_v.reshape(1, flat), w1_v, precision=HI)
    const = lambda shp: pl.BlockSpec(shp, lambda b, g: (0,) * len(shp))
    return pl.pallas_call(
        _cmp_body,
        grid=(B, G),
        in_specs=[pl.BlockSpec((None, None, M, CMP_STRIDE * LANES), lambda b, g: (b, g, 0, 0)),
                  const((CMP_LEN * LANES, CMP_HIDDEN)), const((CMP_LEN * LANES, CMP_HIDDEN)),
                  const((1, CMP_HIDDEN)), const((1, CMP_HIDDEN)),
                  const((CMP_HIDDEN, Dh)), const((CMP_HIDDEN, Dh))],
        out_specs=[pl.BlockSpec((None, None, M, LANES), lambda b, g: (b, g, 0, 0)),
                   pl.BlockSpec((None, None, Dh, M), lambda b, g: (b, g, 0, 0))],
        out_shape=[jax.ShapeDtypeStruct((B, G, M, LANES), bf16), jax.ShapeDtypeStruct((B, G, Dh, M), bf16)],
        compiler_params=_cparams("parallel", "parallel"),
        name="nsa_compress",
    )(x2, w1k_e, w1v_e, c0k, c0v, w2_k.astype(bf16), w2_v.astype(bf16))


def _nsa_body(q_ref, kc_ref, vct_ref, ks_ref, vst_ref, kw_ref, vwt_ref, gl_ref, ovt_ref,
              o_ref, score_ref, cnt_ref, q3_ref, selg_ref, s_sel, p_sel, acc_sel, s_win, p_win, acc_win):
    QT = NSA_QT
    KB = NSA_KB
    R = NSA_Q_PER_KV
    Dh = HEAD_DIM
    T = ks_ref.shape[0]
    NB = T // SEL_LEN
    NC = kc_ref.shape[0]
    qi = pl.program_id(2)
    t0 = qi * QT

    W = R * QT
    q3_ref[...] = jnp.concatenate([q_ref[r * LANES:(r + 1) * LANES, :] for r in range(R)], axis=1)
    slope_rows = q3_ref[Dh:Dh + NSA_AUG, :].astype(f32)[0:4]
    tq1 = t0 + lax.broadcasted_iota(jnp.int32, (1, QT), 1)
    tile3 = lambda t: jnp.concatenate([t] * R, axis=1)

    def col_reduce(x, op):
        parts = [x[i * SUBLANES:(i + 1) * SUBLANES] for i in range(x.shape[0] // SUBLANES)]
        while len(parts) > 1:
            nxt = [op(parts[2 * i], parts[2 * i + 1]) for i in range(len(parts) // 2)]
            parts = nxt + parts[2 * (len(parts) // 2):]
        return parts[0]

    colmax = lambda x: jnp.max(col_reduce(x, jnp.maximum), axis=0, keepdims=True)
    colsum = lambda x: jnp.sum(col_reduce(x, jnp.add), axis=0, keepdims=True)

    n_c = lax.broadcasted_iota(jnp.int32, (NC, 1), 0)
    vis_c = jnp.where(((n_c * CMP_STRIDE + (CMP_LEN - 1)) <= tq1) & (n_c < NC - 1), 0.0, NEG_INF)
    s_c = jnp.dot(kc_ref[...], q3_ref[...], preferred_element_type=f32) + tile3(vis_c)
    m_c = colmax(s_c)
    e_c = jnp.exp2(s_c - m_c)
    p_c = e_c * jnp.where(m_c > 0.5 * NEG_INF, 1.0 / colsum(e_c), 0.0)
    o_c = jnp.dot(vct_ref[...], p_c.astype(bf16), preferred_element_type=f32)
    p_sum = p_c[:, 0:QT]
    for r in range(1, R):
        p_sum = p_sum + p_c[:, r * QT:(r + 1) * QT]
    imp = _split_dot(p_sum, ovt_ref[...], 3, True)

    j = lax.broadcasted_iota(jnp.int32, (NB, 1), 0)
    cur = tq1 >> (SEL_LEN.bit_length() - 1)
    valid = (j * SEL_LEN) <= tq1
    forced = (j == 0) | (j == cur) | (j == cur - 1)
    score_ref[...] = jnp.where(valid, imp + FORCED_BONUS * forced.astype(f32), -jnp.inf)
    cnt_ref[...] = jnp.zeros_like(cnt_ref)
    nrb = NB // SUBLANES
    jsub = lax.broadcasted_iota(jnp.int32, (SUBLANES, 1), 0)
    last_rb = (t0 + QT - 1) // (SUBLANES * SEL_LEN)
    for ib in range(nrb):
        @pl.when(ib <= last_rb)
        def _(ib=ib):
            rows = [jnp.broadcast_to(score_ref[ib * SUBLANES + ii:ib * SUBLANES + ii + 1, :], (SUBLANES, QT))
                    for ii in range(SUBLANES)]
            for jb in range(nrb):
                @pl.when(jb <= last_rb)
                def _(jb=jb):
                    sj = score_ref[jb * SUBLANES:(jb + 1) * SUBLANES, :]
                    acc = cnt_ref[jb * SUBLANES:(jb + 1) * SUBLANES, :]
                    for ii in range(SUBLANES):
                        if ib < jb:
                            ahead = rows[ii] >= sj
                        elif ib > jb:
                            ahead = rows[ii] > sj
                        else:
                            ahead = (rows[ii] > sj) | ((rows[ii] == sj) & (ii < jsub))
                        acc = acc + jnp.where(ahead, 1.0, 0.0)
                    cnt_ref[jb * SUBLANES:(jb + 1) * SUBLANES, :] = acc
    selected = (cnt_ref[...] < float(min(SEL_TOPK, NB))) & valid
    sel_add = tile3(jnp.where(selected, 0.0, NEG_INF))
    spb = KB // SEL_LEN
    pad_rows = jnp.zeros((NSA_AUG - 4 - spb, W), f32)
    for tb in range(T // KB):
        selg_ref[tb] = jnp.concatenate([slope_rows, sel_add[tb * spb:(tb + 1) * spb], pad_rows], axis=0).astype(bf16)

    def stream(k_ref, vt_ref, n_plain, n_masked, start_of, mask_of, before_scores, s_buf, p_buf, acc_ref):
        n_all = n_plain + n_masked

        def scores(i, slot):
            ic = jnp.minimum(i, n_all - 1)
            before_scores(ic)
            s_buf[slot] = jnp.dot(k_ref[pl.ds(start_of(ic), KB), :], q3_ref[...], preferred_element_type=f32)

        def body(i, carry, masked, slot):
            m, l, alpha_prev = carry
            vblk = vt_ref[:, pl.ds(start_of(jnp.maximum(i - 1, 0)), KB)]
            pv = jnp.dot(vblk, p_buf[1 - slot], preferred_element_type=f32)
            scores(i + 1, 1 - slot)
            s = s_buf[slot]
            if masked:
                s = s + tile3(mask_of(i))
            m_new = jnp.maximum(m, colmax(s))
            alpha = jnp.exp2(m - m_new)
            p = jnp.exp2(s - m_new)
            p_buf[slot] = p.astype(bf16)
            acc_ref[...] = alpha_prev * acc_ref[...] + pv
            return m_new, alpha * l + colsum(p), alpha

        p_buf[1] = jnp.zeros((KB, W), bf16)
        acc_ref[...] = jnp.zeros_like(acc_ref)
        scores(0, 0)
        carry = (jnp.full((1, W), NEG_INF, f32), jnp.zeros((1, W), f32), jnp.ones((1, W), f32))

        def pair(j, carry, masked):
            return body(2 * j + 1, body(2 * j, carry, masked, 0), masked, 1)

        carry = lax.fori_loop(0, n_plain // 2, functools.partial(pair, masked=False), carry)
        _, l, alpha_last = lax.fori_loop(n_plain // 2, n_all // 2, functools.partial(pair, masked=True), carry)
        pv = jnp.dot(vt_ref[:, pl.ds(start_of(n_all - 1), KB)], p_buf[1], preferred_element_type=f32)
        return (alpha_last * acc_ref[...] + pv) * (1.0 / l)

    rel = lax.broadcasted_iota(jnp.int32, (1, QT), 1) - lax.broadcasted_iota(jnp.int32, (KB, 1), 0)

    nfull = qi * (QT // KB)

    def load_sel_rows(i):
        q3_ref[Dh:Dh + NSA_AUG, :] = selg_ref[i]

    o_s = stream(ks_ref, vst_ref, nfull, QT // KB,
                 lambda i: pl.multiple_of(i * KB, KB),
                 lambda i: jnp.where(rel - (i - nfull) * KB >= 0, 0.0, NEG_INF),
                 load_sel_rows, s_sel, p_sel, acc_sel)

    def win_mask(i):
        off = i * KB - WINDOW
        d = rel - off
        return jnp.where((d >= 0) & (d < WINDOW) & (t0 + off >= 0), 0.0, NEG_INF)

    o_w = stream(kw_ref, vwt_ref, 0, (WINDOW + QT) // KB,
                 lambda i: pl.multiple_of(jnp.maximum(t0 - WINDOW + i * KB, 0), KB),
                 win_mask, lambda i: None, s_win, p_win, acc_win)


    gates = _sigmoid(gl_ref[...])
    for r in range(R):
        ls = slice(r * QT, (r + 1) * QT)
        o_ref[r * Dh:(r + 1) * Dh, :] = (gates[3 * r:3 * r + 1, :] * o_c[:, ls]
                                          + gates[3 * r + 1:3 * r + 2, :] * o_s[:, ls]
                                          + gates[3 * r + 2:3 * r + 3, :] * o_w[:, ls]).astype(o_ref.dtype)


def _alibi_slopes(n):
    def pow2(m):
        start = 2.0 ** (-8.0 / m)
        return [start ** (i + 1) for i in range(m)]
    if math.log2(n).is_integer():
        return pow2(n)
    c = 2 ** math.floor(math.log2(n))
    return pow2(c) + pow2(2 * c)[0::2][: n - c]


def _nsa_attend(q_t, k_cmp, v_cmp_t, ks, vs_t, kw, vw_t, gl_t):
    B, G, T, _ = ks.shape
    Dh = HEAD_DIM
    QT = NSA_QT
    R = NSA_Q_PER_KV
    NB = T // SEL_LEN
    NC = k_cmp.shape[2]
    assert QT % NSA_KB == 0 and WINDOW % NSA_KB == 0 and T % QT == 0 and NSA_KB % SEL_LEN == 0
    n = np.arange(NC)
    jb = np.arange(NB)
    ov = ((n[:, None] * CMP_STRIDE < jb[None, :] * SEL_LEN + SEL_LEN)
          & (n[:, None] * CMP_STRIDE + CMP_LEN - 1 >= jb[None, :] * SEL_LEN) & (n[:, None] < NC - 1))
    ovt = jnp.asarray(ov.T.astype(np.float32))
    per_bg = lambda shp: pl.BlockSpec((None, None) + shp, lambda b, g, i: (b, g, 0, 0))
    const = lambda shp: pl.BlockSpec(shp, lambda b, g, i: (0, 0))
    return pl.pallas_call(
        _nsa_body,
        grid=(B, G, T // QT),
        in_specs=[pl.BlockSpec((None, R * LANES, QT), lambda b, g, i: (b, g, i)),
                  per_bg((NC, LANES)), per_bg((Dh, NC)),
                  per_bg((T, LANES)), per_bg((Dh, T)), per_bg((T, LANES)), per_bg((Dh, T)),
                  pl.BlockSpec((None, GL_ROWS, QT), lambda b, g, i: (b, g, i)),
                  const((NB, NC))],
        out_specs=pl.BlockSpec((None, R * Dh, QT), lambda b, g, i: (b, g, i)),
        out_shape=jax.ShapeDtypeStruct((B, G * R * Dh, T), bf16),
        scratch_shapes=[pltpu.VMEM((NB, QT), f32), pltpu.VMEM((NB, QT), f32), pltpu.VMEM((LANES, R * QT), bf16),
                        pltpu.VMEM((T // NSA_KB, NSA_AUG, R * QT), bf16),
                        pltpu.VMEM((2, NSA_KB, R * QT), f32), pltpu.VMEM((2, NSA_KB, R * QT), bf16),
                        pltpu.VMEM((Dh, R * QT), f32),
                        pltpu.VMEM((2, NSA_KB, R * QT), f32), pltpu.VMEM((2, NSA_KB, R * QT), bf16),
                        pltpu.VMEM((Dh, R * QT), f32)],
        compiler_params=_cparams("parallel", "parallel", "arbitrary"),
        name="nsa_attend",
    )(q_t, k_cmp, v_cmp_t, ks, vs_t, kw, vw_t, gl_t, ovt)


def _s5_body(u_ref, tg_ref, bend_ref, cpow_ref, apow_ref, y_ref):
    B, CH, T = u_ref.shape
    Cs = S5_CHUNK
    nk = T // Cs
    rows = nk * B
    conv = None
    inc = None
    for ci in range(CH):
        a = jnp.concatenate([u_ref[:, ci, k * Cs:(k + 1) * Cs] for k in range(nk)], axis=0).astype(bf16)
        c1 = jnp.dot(a, tg_ref[ci], preferred_element_type=f32)
        i1 = jnp.dot(a, bend_ref[ci], preferred_element_type=f32)
        conv = c1 if conv is None else conv + c1
        inc = i1 if inc is None else inc + i1
    kidx = lax.broadcasted_iota(jnp.int32, (rows, 1), 0) >> (B.bit_length() - 1)
    x = inc
    sh = 1
    lvl = 0
    while sh < nk:
        prev = jnp.where(kidx >= sh, pltpu.roll(x, sh * B, 0), 0.0)
        x = x + apow_ref[lvl, 0:1, :] * prev + apow_ref[lvl, 1:2, :] * pltpu.roll(prev, S5_STATE, 1)
        sh *= 2
        lvl += 1
    start = jnp.where(kidx >= 1, pltpu.roll(x, B, 0), 0.0)
    y = conv + jnp.dot(start.astype(bf16), cpow_ref[...], preferred_element_type=f32)
    for k in range(nk):
        for co in range(CH):
            y_ref[:, co, k * Cs:(k + 1) * Cs] = y[k * B:(k + 1) * B, co * Cs:(co + 1) * Cs]


def _s5_tables(lam_re, lam_im, log_dt, b_re, b_im, c_re, c_im):
    Cs = S5_CHUNK
    lam_re, lam_im = lam_re.astype(f32), lam_im.astype(f32)
    b_re, b_im, c_re, c_im = (t.astype(f32) for t in (b_re, b_im, c_re, c_im))
    dt = jnp.exp(log_dt.astype(f32))[:, None]
    mag = jnp.exp(lam_re * dt)
    ab_re, ab_im = mag * jnp.cos(lam_im * dt), mag * jnp.sin(lam_im * dt)
    den = lam_re * lam_re + lam_im * lam_im
    f_re = ((ab_re - 1.0) * lam_re + ab_im * lam_im) / den
    f_im = (ab_im * lam_re - (ab_re - 1.0) * lam_im) / den
    bb_re = f_re[..., None] * b_re - f_im[..., None] * b_im
    bb_im = f_re[..., None] * b_im + f_im[..., None] * b_re
    tau = jnp.arange(Cs + 1, dtype=f32)[None, None, :]
    pmag = jnp.exp(lam_re[..., None] * dt[..., None] * tau)
    pang = lam_im[..., None] * dt[..., None] * tau
    pw_re, pw_im = pmag * jnp.cos(pang), pmag * jnp.sin(pang)
    ein = functools.partial(jnp.einsum, precision=HI)
    ca_re = c_re[..., None] * pw_re[:, None] - c_im[..., None] * pw_im[:, None]
    ca_im = c_re[..., None] * pw_im[:, None] + c_im[..., None] * pw_re[:, None]
    kern = ein('gopt,gpi->gtoi', ca_re[..., :Cs], bb_re) - ein('gopt,gpi->gtoi', ca_im[..., :Cs], bb_im)
    s_idx = jnp.arange(Cs)
    lag_np = np.arange(Cs)[None, :] - np.arange(Cs)[:, None]
    place = jnp.asarray((lag_np[:, :, None] == np.arange(Cs)[None, None, :]).astype(np.float32)).astype(bf16)
    tg = jnp.einsum('stl,gloi->gisot', place, kern.astype(bf16), preferred_element_type=bf16)
    tg = tg.reshape(S5_GROUPS, S5_CH, Cs, S5_CH * Cs)
    rev = (Cs - 1) - s_idx
    pe_re, pe_im = pw_re[:, :, rev], pw_im[:, :, rev]
    be_re = pe_re[..., None] * bb_re[:, :, None, :] - pe_im[..., None] * bb_im[:, :, None, :]
    be_im = pe_re[..., None] * bb_im[:, :, None, :] + pe_im[..., None] * bb_re[:, :, None, :]
    bend = jnp.concatenate([jnp.transpose(be_re, (0, 3, 2, 1)), jnp.transpose(be_im, (0, 3, 2, 1))], axis=-1)
    cp_re = jnp.transpose(ca_re[..., 1:], (0, 2, 1, 3))
    cp_im = jnp.transpose(ca_im[..., 1:], (0, 2, 1, 3))
    cpow = jnp.concatenate([cp_re, -cp_im], axis=1).reshape(S5_GROUPS, 2 * S5_STATE, S5_CH * Cs)
    lv = (Cs * 2.0 ** jnp.arange(S5_SCAN_LEVELS, dtype=f32))[None, None, :]
    qmag = jnp.exp(lam_re[..., None] * dt[..., None] * lv)
    qang = lam_im[..., None] * dt[..., None] * lv
    q_re, q_im = jnp.transpose(qmag * jnp.cos(qang), (0, 2, 1)), jnp.transpose(qmag * jnp.sin(qang), (0, 2, 1))
    apow = jnp.stack([jnp.concatenate([q_re, q_re], axis=-1), jnp.concatenate([-q_im, q_im], axis=-1)], axis=2)
    return tg.astype(bf16), bend.astype(bf16), cpow.astype(bf16), apow


def _s5_scan(u_t, tables):
    B, _, T = u_t.shape
    Cs = S5_CHUNK
    nk = T // Cs
    assert B == SUBLANES and nk & (nk - 1) == 0 and nk <= 2 ** S5_SCAN_LEVELS
    tg, bend, cpow, apow = tables
    W = S5_CH * Cs
    per_g = lambda shp: pl.BlockSpec((None,) + shp, lambda g: (g,) + (0,) * len(shp))
    grp = pl.BlockSpec((B, S5_CH, T), lambda g: (0, g, 0))
    return pl.pallas_call(
        _s5_body,
        grid=(S5_GROUPS,),
        in_specs=[grp, per_g((S5_CH, Cs, W)), per_g((S5_CH, Cs, 2 * S5_STATE)), per_g((2 * S5_STATE, W)),
                  per_g((S5_SCAN_LEVELS, 2, 2 * S5_STATE))],
        out_specs=grp,
        out_shape=jax.ShapeDtypeStruct((B, S5_W, T), f32),
        compiler_params=_cparams("parallel"),
        name="s5_conv",
    )(u_t, tg, bend, cpow, apow)


def _s5_post_body(y_ref, u_ref, d_ref, w_ref, o_ref):
    y = _gelu_tanh(y_ref[...] + d_ref[...] * u_ref[...])
    vg = _dot_tn(y, w_ref[...])
    o_ref[...] = vg[:, 0:S5_W] * _sigmoid(vg[:, S5_W:2 * S5_W])


def _s5_post(y_t, u_t, d_skip, w_glu):
    B, _, T = y_t.shape
    tm = min(S5P_TM, T)
    blk = pl.BlockSpec((None, S5_W, tm), lambda b, i: (b, 0, i))
    return pl.pallas_call(
        _s5_post_body,
        grid=(B, T // tm),
        in_specs=[blk, blk, pl.BlockSpec((S5_W, 1), lambda b, i: (0, 0)),
                  pl.BlockSpec((S5_W, 2 * S5_W), lambda b, i: (0, 0))],
        out_specs=pl.BlockSpec((None, tm, S5_W), lambda b, i: (b, i, 0)),
        out_shape=jax.ShapeDtypeStruct((B, T, S5_W), f32),
        compiler_params=_cparams("parallel", "parallel"),
        name="s5_glu",
    )(y_t, u_t, d_skip.reshape(S5_W, 1), w_glu.astype(bf16))


def _out_proj_body(h_ref, orw_ref, onsat_ref, os5_ref, w_ref, g_ref, o_ref):
    mix = (_dot(orw_ref[...], w_ref[0:RWKV_W, :])
           + _dot_tn(onsat_ref[...], w_ref[RWKV_W:RWKV_W + NSA_W, :])
           + _dot(os5_ref[...], w_ref[RWKV_W + NSA_W:D_MIX, :]))
    o_ref[...] = h_ref[...] + _rms(mix, g_ref[...])


def _out_proj(h, o_rw, o_nsa_t, o_s5, w_out, g):
    B, T, _ = h.shape
    tm = min(OUT_TM, T)
    rows = lambda w: pl.BlockSpec((None, tm, w), lambda b, i: (b, i, 0))
    return pl.pallas_call(
        _out_proj_body,
        grid=(B, T // tm),
        in_specs=[rows(D_MODEL), rows(RWKV_W), pl.BlockSpec((None, NSA_W, tm), lambda b, i: (b, 0, i)), rows(S5_W),
                  pl.BlockSpec((D_MIX, D_MODEL), lambda b, i: (0, 0)),
                  pl.BlockSpec((1, D_MODEL), lambda b, i: (0, 0))],
        out_specs=rows(D_MODEL),
        out_shape=jax.ShapeDtypeStruct((B, T, D_MODEL), f32),
        compiler_params=_cparams("parallel", "parallel"),
        name="out_proj",
    )(h, o_rw, o_nsa_t, o_s5, w_out.astype(bf16), g.reshape(1, D_MODEL))


def _ffn_body(h_ref, hp_ref, gpre_ref, wg_ref, wu_ref, cwg_ref, cwu_ref, cbg_ref, cbu_ref, wd_ref, gpost_ref,
              o_ref, xn_ref, acc_ref, *, tiles_per_seq):
    tm = h_ref.shape[0]
    H = SUBLANES
    c = pl.program_id(1)

    @pl.when(c == 0)
    def _():
        first = (pl.program_id(0) % tiles_per_seq) == 0
        xp = _rms(hp_ref[...], gpre_ref[...])
        xn_ref[0:H, :] = jnp.where(first, 0.0, xp).astype(bf16)
        xn_ref[H:H + tm, :] = _rms(h_ref[...], gpre_ref[...]).astype(bf16)
        acc_ref[...] = jnp.zeros_like(acc_ref)

    xn = xn_ref[...]

    def conv_branch(w_ref, cw_ref, cb_ref):
        hu = jnp.dot(xn, w_ref[...], preferred_element_type=f32)
        cw = cw_ref[...]
        out = (cw[0:1, :] * pltpu.roll(hu, 2, 0)[H:H + tm]
               + cw[1:2, :] * pltpu.roll(hu, 1, 0)[H:H + tm]
               + cw[2:3, :] * hu[H:H + tm])
        return out + cb_ref[...]

    gate = conv_branch(wg_ref, cwg_ref, cbg_ref)
    up = conv_branch(wu_ref, cwu_ref, cbu_ref)
    acc_ref[...] += _dot(_gelu_tanh(gate) * up, wd_ref[...])

    @pl.when(c == pl.num_programs(1) - 1)
    def _():
        o_ref[...] = h_ref[...] + _rms(acc_ref[...], gpost_ref[...])


def _conv_ffn(h2d, seq_len, g_pre, w_up, conv_w, conv_b, w_down, g_post):
    n = h2d.shape[0]
    tm = min(FFN_TM, seq_len)
    tc = FFN_TC
    nc = D_FF // tc
    H = SUBLANES
    tiles_per_seq = seq_len // tm
    hpb = tm // H
    w_up = w_up.astype(bf16)
    cb = conv_b.reshape(1, 2 * D_FF)
    return pl.pallas_call(
        functools.partial(_ffn_body, tiles_per_seq=tiles_per_seq),
        grid=(n // tm, nc),
        in_specs=[pl.BlockSpec((tm, D_MODEL), lambda i, c: (i, 0)),
                  pl.BlockSpec((H, D_MODEL), lambda i, c: (jnp.maximum(i * hpb - 1, 0), 0)),
                  pl.BlockSpec((1, D_MODEL), lambda i, c: (0, 0)),
                  pl.BlockSpec((D_MODEL, tc), lambda i, c: (0, c)),
                  pl.BlockSpec((D_MODEL, tc), lambda i, c: (0, nc + c)),
                  pl.BlockSpec((CONV_W, tc), lambda i, c: (0, c)),
                  pl.BlockSpec((CONV_W, tc), lambda i, c: (0, nc + c)),
                  pl.BlockSpec((1, tc), lambda i, c: (0, c)),
                  pl.BlockSpec((1, tc), lambda i, c: (0, nc + c)),
                  pl.BlockSpec((tc, D_MODEL), lambda i, c: (c, 0)),
                  pl.BlockSpec((1, D_MODEL), lambda i, c: (0, 0))],
        out_specs=pl.BlockSpec((tm, D_MODEL), lambda i, c: (i, 0)),
        out_shape=jax.ShapeDtypeStruct((n, D_MODEL), f32),
        scratch_shapes=[pltpu.VMEM((H + tm, D_MODEL), bf16), pltpu.VMEM((tm, D_MODEL), f32)],
        compiler_params=_cparams("parallel", "arbitrary"),
        name="conv_ffn",
    )(h2d, h2d, g_pre.reshape(1, D_MODEL), w_up, w_up, conv_w, conv_w, cb, cb, w_down.astype(bf16),
      g_post.reshape(1, D_MODEL))


def _ple_body(h_ref, p_ref, wp_ref, gp_ref, wg_ref, o_ref):
    h = h_ref[...]
    e = _rms(_dot(p_ref[...], wp_ref[...]), gp_ref[...])
    gate = _sigmoid(_dot(h, wg_ref[...]))
    o_ref[...] = h + gate * e


def _ple(h2d, p2d, w_ple, g_ple, w_gate):
    n = h2d.shape[0]
    tm = min(PLE_TM, n)
    return pl.pallas_call(
        _ple_body,
        grid=(n // tm,),
        in_specs=[pl.BlockSpec((tm, D_MODEL), lambda i: (i, 0)),
                  pl.BlockSpec((tm, PLE_DIM), lambda i: (i, 0)),
                  pl.BlockSpec((PLE_DIM, D_MODEL), lambda i: (0, 0)),
                  pl.BlockSpec((1, D_MODEL), lambda i: (0, 0)),
                  pl.BlockSpec((D_MODEL, D_MODEL), lambda i: (0, 0))],
        out_specs=pl.BlockSpec((tm, D_MODEL), lambda i: (i, 0)),
        out_shape=jax.ShapeDtypeStruct((n, D_MODEL), f32),
        compiler_params=_cparams("parallel"),
        name="ple_gate",
    )(h2d, p2d, w_ple.astype(bf16), g_ple.reshape(1, D_MODEL), w_gate.astype(bf16))


def kernel(x, p, pre_mix_norm, post_mix_norm, pre_ffn_norm, post_ffn_norm, w_in, w_out, shift_mu, rw_w0, rw_w2, rw_a0, rw_a2, rw_g2, rw_k_k, rw_k_a, rw_r_k, rw_gn_w, rw_gn_b, cmp_pe_k, cmp_pe_v, cmp_w1_k, cmp_w2_k, cmp_w1_v, cmp_w2_v, s5_lam_re, s5_lam_im, s5_log_dt, s5_b_re, s5_b_im, s5_c_re, s5_c_im, s5_d, s5_w_glu, w_up, conv_w, conv_b, w_down, w_ple, ple_norm, w_ple_gate):
    B, T, D = x.shape
    n = B * T
    depth = w_in.shape[0]
    h = x
    for i in range(depth):
        wn, wt, qaug, mu_p = _prep_in_weights(w_in[i], shift_mu[i])
        z_rw, z_s5, ks, kw, kcv, q_t, vs_t, vw_t, gl_t = _in_proj(h, pre_mix_norm[i], wn, wt, qaug)
        o_rw = _rwkv(z_rw, mu_p, rw_w0[i], rw_w2[i], rw_a0[i], rw_a2[i], rw_g2[i],
                     rw_k_k[i], rw_k_a[i], rw_r_k[i], rw_gn_w[i], rw_gn_b[i])
        k_cmp, v_cmp_t = _compress(kcv, cmp_pe_k[i], cmp_pe_v[i], cmp_w1_k[i], cmp_w2_k[i], cmp_w1_v[i], cmp_w2_v[i])
        o_nsa_t = _nsa_attend(q_t, k_cmp, v_cmp_t, ks, vs_t, kw, vw_t, gl_t)
        tables = _s5_tables(s5_lam_re[i], s5_lam_im[i], s5_log_dt[i], s5_b_re[i], s5_b_im[i], s5_c_re[i], s5_c_im[i])
        o_s5 = _s5_post(_s5_scan(z_s5, tables), z_s5, s5_d[i], s5_w_glu[i])
        h = _out_proj(h, o_rw, o_nsa_t, o_s5, w_out[i], post_mix_norm[i])
        h = _conv_ffn(h.reshape(n, D), T, pre_ffn_norm[i], w_up[i], conv_w[i], conv_b[i], w_down[i], post_ffn_norm[i])
        h = _ple(h, p[i].reshape(n, PLE_DIM), w_ple[i], ple_norm[i], w_ple_gate[i]).reshape(B, T, D)
    return h
```

```python
import functools
import math

import numpy as np
import jax
import jax.numpy as jnp
from jax import lax
from jax.experimental import pallas as pl
from jax.experimental.pallas import tpu as pltpu

f32 = jnp.float32
bf16 = jnp.bfloat16
HI = lax.Precision.HIGHEST

D_MODEL = 1024
HEAD_DIM = 64
RWKV_HEADS = 6
RWKV_W = RWKV_HEADS * HEAD_DIM
D_DECAY_LORA = 64
D_AAA_LORA = 64
D_GATE_LORA = 128
RWKV_COLS = 3 * RWKV_W + D_DECAY_LORA + D_AAA_LORA + D_GATE_LORA
GN_EPS = 64e-5
NSA_Q_HEADS = 6
NSA_KV_GROUPS = 2
NSA_Q_PER_KV = NSA_Q_HEADS // NSA_KV_GROUPS
NSA_W = NSA_Q_HEADS * HEAD_DIM
NSA_KV_W = NSA_KV_GROUPS * HEAD_DIM
NSA_N_BRANCH = 3
NSA_COLS = NSA_W + 6 * NSA_KV_W + NSA_Q_HEADS * NSA_N_BRANCH
CMP_LEN = 32
CMP_STRIDE = 16
CMP_HIDDEN = 128
SEL_LEN = 64
SEL_TOPK = 16
WINDOW = 512
FORCED_BONUS = 1e3
NEG_INF = -1e30
S5_GROUPS = 16
S5_CH = 16
S5_W = S5_GROUPS * S5_CH
S5_STATE = 64
D_MIX = RWKV_W + NSA_W + S5_W
D_FF = 2816
CONV_W = 3
PLE_DIM = 256
NORM_EPS = 1e-6

LANES = 128
SUBLANES = 8
VMEM_LIMIT = 56 * 1024 * 1024

RW_LORA_W_OFF = 3 * RWKV_W
RW_LORA_A_OFF = RW_LORA_W_OFF + LANES
RW_LORA_G_OFF = RW_LORA_A_OFF + LANES
RW_P = RW_LORA_G_OFF + D_GATE_LORA
NSA_GATE_OFF = NSA_W + 6 * NSA_KV_W
NAT_KSW = RW_P
NAT_KCV = NAT_KSW + NSA_KV_GROUPS * 2 * LANES
NAT_P = NAT_KCV + NSA_KV_GROUPS * LANES
GL_ROWS = 16
TR_VS = NSA_Q_HEADS * LANES
TR_VW = TR_VS + NSA_KV_W
TR_GL = TR_VW + NSA_KV_W
TR_S5 = TR_GL + NSA_KV_GROUPS * GL_ROWS
TR_P = TR_S5 + S5_W

IN_TM = 512
RW_CHUNK = 64
RW_NCH = 4
NSA_QT = 512
NSA_KB = 256
NSA_AUG = 16
S5_CHUNK = 64
S5_SCAN_LEVELS = 8
OUT_TM = 1024
FFN_TM = 512
FFN_TC = 1408
PLE_TM = 1024
S5P_TM = 1024


def _cparams(*sem):
    return pltpu.CompilerParams(dimension_semantics=sem, vmem_limit_bytes=VMEM_LIMIT)


def _rms(x, g):
    ms = jnp.mean(x * x, axis=-1, keepdims=True)
    return x * lax.rsqrt(ms + NORM_EPS) * g


def _gelu_tanh(x):
    return 0.5 * x * (1.0 + jnp.tanh(math.sqrt(2.0 / math.pi) * (x + 0.044715 * (x * x * x))))


def _sigmoid(x):
    return 1.0 / (1.0 + jnp.exp(-x))


def _softplus(x):
    return jnp.maximum(x, 0.0) + jnp.log(1.0 + jnp.exp(-jnp.abs(x)))


def _dot(a, b):
    return jnp.dot(a.astype(bf16), b.astype(bf16), preferred_element_type=f32)


def _dot_nt(a, b):
    return lax.dot_general(a.astype(bf16), b.astype(bf16), (((1,), (1,)), ((), ())), preferred_element_type=f32)


def _dot_tn(a, b):
    return lax.dot_general(a.astype(bf16), b.astype(bf16), (((0,), (0,)), ((), ())), preferred_element_type=f32)


def _dot_hi(a, b):
    return jnp.dot(a, b, preferred_element_type=f32, precision=HI)


def _split_dot(x, m01, terms, m_left):
    m = m01.astype(bf16)
    acc = None
    rem = x
    for _ in range(terms):
        piece = rem.astype(bf16)
        d = (jnp.dot(m, piece, preferred_element_type=f32) if m_left
             else jnp.dot(piece, m, preferred_element_type=f32))
        acc = d if acc is None else acc + d
        rem = rem - piece.astype(f32)
    return acc


def _key_tail(t, sel_onehot):
    col = lax.broadcasted_iota(jnp.int32, t.shape[:1] + (LANES,), 1) - HEAD_DIM
    lo = (t & (LANES - 1)).astype(f32)
    hi = (t >> (LANES.bit_length() - 1)).astype(f32)
    tail = jnp.where((col == 0) | (col == 2), lo, jnp.where((col == 1) | (col == 3), hi, 0.0))
    if sel_onehot:
        spb = NSA_KB // SEL_LEN
        blk = (t >> (SEL_LEN.bit_length() - 1)) & (spb - 1)
        tail = tail + jnp.where(col - 4 == blk, 1.0, 0.0) * jnp.where((col >= 4) & (col < 4 + spb), 1.0, 0.0)
    return tail


def _in_proj_body(x_ref, g_ref, wn_ref, wt_ref, qaug_ref,
                  zrw_ref, zs5_ref, ks_ref, kw_ref, kcv_ref, qt_ref, vst_ref, vwt_ref, glt_ref, kcv_scr):
    tm = x_ref.shape[0]
    G, Dh = NSA_KV_GROUPS, HEAD_DIM
    xn = _rms(x_ref[...], g_ref[...]).astype(bf16)
    zrw_ref[...] = jnp.dot(xn, wn_ref[:, 0:RW_P], preferred_element_type=f32)
    ksw =jnp.dot(xn, wn_ref[:, NAT_KSW:NAT_KCV], preferred_element_type=f32)
    tok = pl.program_id(1) * tm + lax.broadcasted_iota(jnp.int32, (tm, 1), 0)
    tail_w = _key_tail(tok, False)
    tail_s = _key_tail(tok, True)
    for g in range(G):
        ks_ref[g] = (ksw[:, 2 * g * LANES:(2 * g + 1) * LANES] + tail_s).astype(bf16)
        kw_ref[g] = (ksw[:, (2 * g + 1) * LANES:(2 * g + 2) * LANES] + tail_w).astype(bf16)
    kcv = jnp.dot(xn, wn_ref[:, NAT_KCV:NAT_P], preferred_element_type=f32)
    for g in range(G):
        kcv_scr[...] = kcv[:, g * LANES:(g + 1) * LANES]
        kcv_ref[g] = jnp.concatenate([kcv_scr[pl.ds(l, tm // CMP_STRIDE, stride=CMP_STRIDE), :]
                                      for l in range(CMP_STRIDE)], axis=1).astype(bf16)
    zt = lax.dot_general(wt_ref[...], xn, (((1,), (1,)), ((), ())), preferred_element_type=f32)
    qscale = HEAD_DIM ** -0.5 * math.log2(math.e)
    qt_ref[...] = (zt[0:TR_VS] * qscale + qaug_ref[...]).astype(bf16)
    for g in range(G):
        vst_ref[g] = zt[TR_VS + g * Dh:TR_VS + (g + 1) * Dh].astype(bf16)
        vwt_ref[g] = zt[TR_VW + g * Dh:TR_VW + (g + 1) * Dh].astype(bf16)
    glt_ref[...] = zt[TR_GL:TR_S5]
    zs5_ref[...] = zt[TR_S5:TR_P]


def _in_proj(h, g, wn, wt, qaug):
    B, T, _ = h.shape
    tm = min(IN_TM, T)
    G, Dh = NSA_KV_GROUPS, HEAD_DIM
    const = lambda shp: pl.BlockSpec(shp, lambda b, i: (0,) * len(shp))
    tok3 = lambda w: pl.BlockSpec((None, tm, w), lambda b, i: (b, i, 0))
    tok4 = lambda w: pl.BlockSpec((None, G, tm, w), lambda b, i: (b, 0, i, 0))
    trn3 = lambda r: pl.BlockSpec((None, r, tm), lambda b, i: (b, 0, i))
    trn4 = pl.BlockSpec((None, G, Dh, tm), lambda b, i: (b, 0, 0, i))
    S = jax.ShapeDtypeStruct
    return pl.pallas_call(
        _in_proj_body,
        grid=(B, T // tm),
        in_specs=[tok3(D_MODEL), const((1, D_MODEL)), const((D_MODEL, NAT_P)), const((TR_P, D_MODEL)),
                  const((TR_VS, 1))],
        out_specs=[tok3(RW_P), trn3(S5_W), tok4(LANES), tok4(LANES),
                   pl.BlockSpec((None, G, tm // CMP_STRIDE, CMP_STRIDE * LANES), lambda b, i: (b, 0, i, 0)),
                   trn3(TR_VS), trn4, trn4, trn3(TR_S5 - TR_GL)],
        out_shape=[S((B, T, RW_P), f32), S((B, S5_W, T), f32), S((B, G, T, LANES), bf16), S((B, G, T, LANES), bf16),
                   S((B, G, T // CMP_STRIDE, CMP_STRIDE * LANES), bf16), S((B, TR_VS, T), bf16),
                   S((B, G, Dh, T), bf16), S((B, G, Dh, T), bf16), S((B, TR_S5 - TR_GL, T), f32)],
        scratch_shapes=[pltpu.VMEM((tm, LANES), f32)],
        compiler_params=_cparams("parallel", "parallel"),
        name="in_proj",
    )(h, g.reshape(1, D_MODEL), wn, wt, qaug)


def _prep_in_weights(w_in, shift_mu):
    d = w_in.shape[0]
    G, R, Dh = NSA_KV_GROUPS, NSA_Q_PER_KV, HEAD_DIM
    zc = lambda k: jnp.zeros((d, k), w_in.dtype)
    o = 3 * RWKV_W
    rw = [w_in[:, :o],
          w_in[:, o:o + D_DECAY_LORA], zc(LANES - D_DECAY_LORA),
          w_in[:, o + D_DECAY_LORA:o + D_DECAY_LORA + D_AAA_LORA], zc(LANES - D_AAA_LORA),
          w_in[:, o + D_DECAY_LORA + D_AAA_LORA:RWKV_COLS]]
    nsa = w_in[:, RWKV_COLS:RWKV_COLS + NSA_COLS]
    s5 = w_in[:, RWKV_COLS + NSA_COLS:]
    sec = lambda i, g: nsa[:, NSA_W + i * NSA_KV_W + g * Dh:NSA_W + i * NSA_KV_W + (g + 1) * Dh]
    ksw = [t for g in range(G) for t in (sec(2, g), zc(LANES - Dh), sec(4, g), zc(LANES - Dh))]
    kcv = [t for g in range(G) for t in (sec(0, g), sec(1, g))]
    wn = jnp.concatenate(rw + ksw + kcv, axis=1).astype(bf16)
    q_cols = [t for h in range(G * R) for t in (nsa[:, h * Dh:(h + 1) * Dh], zc(LANES - Dh))]
    nb = R * NSA_N_BRANCH
    gl_cols = [t for g in range(G) for t in (nsa[:, NSA_GATE_OFF + g * nb:NSA_GATE_OFF + (g + 1) * nb],
                                             zc(GL_ROWS - nb))]
    wt = jnp.concatenate(q_cols + [sec(3, g) for g in range(G)] + [sec(5, g) for g in range(G)] + gl_cols + [s5],
                         axis=1)
    wt = wt.T.astype(bf16)
    sl = np.asarray(_alibi_slopes(NSA_Q_HEADS), np.float64) * math.log2(math.e)
    sl_hi = sl.astype(bf16).astype(np.float64)
    sl_lo = sl - sl_hi
    qaug = np.zeros((G * R, LANES), np.float32)
    qaug[:, Dh:Dh + 4] = np.stack([sl_hi, sl_hi * LANES, sl_lo, sl_lo * LANES], axis=-1)
    z1 = lambda k: jnp.zeros((k,), shift_mu.dtype)
    mu_p = jnp.concatenate([shift_mu[:o],
                            shift_mu[o:o + D_DECAY_LORA], z1(LANES - D_DECAY_LORA),
                            shift_mu[o + D_DECAY_LORA:o + D_DECAY_LORA + D_AAA_LORA], z1(LANES - D_AAA_LORA),
                            shift_mu[o + D_DECAY_LORA + D_AAA_LORA:]])
    return wn, wt, jnp.asarray(qaug.reshape(TR_VS, 1)), mu_p.reshape(1, RW_P)


def _rwkv_body(z_ref, mu_ref, w0_ref, w2_ref, a0_ref, a2_ref, g2_ref, kk_ref, ka_ref, rk_ref, gnw_ref, gnb_ref,
               o_ref, state_ref, prev_ref):
    C = RW_CHUNK
    N = HEAD_DIM
    L = z_ref.shape[0]
    nch = L // C
    npair = RWKV_W // LANES

    @pl.when(pl.program_id(1) == 0)
    def _():
        state_ref[...] = jnp.zeros_like(state_ref)
        prev_ref[...] = jnp.zeros_like(prev_ref)

    z = z_ref[...]
    row = lax.broadcasted_iota(jnp.int32, z.shape, 0)
    zprev = jnp.where(row == 0, prev_ref[0:1, :], pltpu.roll(z, 1, 0))
    prev_ref[0:1, :] = z[L - 1:L, :]
    zs = z + (zprev - z) * mu_ref[...]

    r = zs[:, 0:RWKV_W]
    k = zs[:, RWKV_W:2 * RWKV_W]
    v = zs[:, 2 * RWKV_W:3 * RWKV_W]
    wl = zs[:, RW_LORA_W_OFF:RW_LORA_W_OFF + LANES]
    al = zs[:, RW_LORA_A_OFF:RW_LORA_A_OFF + LANES]
    gl = zs[:, RW_LORA_G_OFF:RW_LORA_G_OFF + D_GATE_LORA]
    lane = lax.broadcasted_iota(jnp.int32, (1, LANES), 1)
    li = lax.broadcasted_iota(jnp.int32, (LANES, LANES), 0)
    lj = lax.broadcasted_iota(jnp.int32, (LANES, LANES), 1)
    hshift = N.bit_length() - 1
    bd128 = ((li >> hshift) == (lj >> hshift)).astype(f32)
    eye128 = (li == lj).astype(f32)
    first_half = lane < N
    lane3 = lax.broadcasted_iota(jnp.int32, (1, RWKV_W), 1)
    half_mask = [(((lane3 >> hshift) & 1) == hh).astype(f32) for hh in range(2)]

    def head_sums(x):
        return jnp.concatenate([_split_dot(x[:, j * LANES:(j + 1) * LANES], bd128, 2, False)
                                for j in range(npair)], axis=1)

    w = -_softplus(-(w0_ref[...] + _dot(jnp.tanh(wl), w2_ref[...]))) - 0.5
    logd = -jnp.exp(w)
    a = _sigmoid(a0_ref[...] + _dot(al, a2_ref[...]))
    g = _dot(_sigmoid(gl), g2_ref[...])
    kkr = k * kk_ref[...]
    kk = kkr / jnp.maximum(jnp.sqrt(head_sums(kkr * kkr)), 1e-12)
    kmod = k * (1.0 + (a - 1.0) * ka_ref[...])
    b = kk * a

    ti = lax.broadcasted_iota(jnp.int32, (C, C), 0)
    si = lax.broadcasted_iota(jnp.int32, (C, C), 1)
    tri_incl = (si <= ti).astype(f32)
    tri_strict = (si < ti).astype(f32)
    eye = (si == ti).astype(f32)
    cums, tots = [], []
    for c in range(nch):
        cc = _split_dot(logd[c * C:(c + 1) * C], tri_incl, 3, True)
        cums.append(cc)
        tots.append(jnp.broadcast_to(cc[C - 1:C], (C, RWKV_W)))
    cum = jnp.concatenate(cums, axis=0)
    tot = jnp.concatenate(tots, axis=0)
    e_inv = jnp.exp(-cum)
    e_end = jnp.exp(tot - cum)
    r_t = r * jnp.exp(cum)
    kk_t = kk * jnp.exp(cum - logd)
    k_h = kmod * e_inv
    b_h = b * e_inv
    k_e = kmod * e_end
    b_e = b * e_end
    e_tot = jnp.exp(tot)
    kk_t_m = [kk_t * m for m in half_mask]
    r_t_m = [r_t * m for m in half_mask]
    k_e_m = [k_e * m for m in half_mask]
    b_e_m = [b_e * m for m in half_mask]

    lvl_masks = []
    s = 1
    while s < C:
        sh = s.bit_length() - 1
        m = ((ti >> (sh + 1)) == (si >> (sh + 1))) & (((ti >> sh) & 1) == 1) & (((si >> sh) & 1) == 0)
        lvl_masks.append(m.astype(f32))
        s *= 2

    chains = [(c, h) for c in range(nch) for h in range(RWKV_HEADS)]

    def blk(arr, c, h):
        j = h // 2
        return arr[c * C:(c + 1) * C, j * LANES:(j + 1) * LANES]

    p = [_dot_nt(jnp.concatenate([blk(kk_t_m[h % 2], c, h), blk(r_t_m[h % 2], c, h)], axis=0),
                 jnp.concatenate([blk(b_h, c, h), blk(k_h, c, h)], axis=0)) for c, h in chains]
    a_ab = [t[0:C, 0:C] * tri_strict for t in p]
    a_ak = [t[0:C, C:2 * C] * tri_strict for t in p]
    a_rb = [t[C:2 * C, 0:C] * tri_incl for t in p]
    a_rk = [t[C:2 * C, C:2 * C] * tri_incl for t in p]
    x = [eye - t * lvl_masks[0] for t in a_ab]
    for m in lvl_masks[1:]:
        t1 = [_dot(xg, ag * m) for xg, ag in zip(x, a_ab)]
        x = [xg - _dot(tg, xg) for xg, tg in zip(x, t1)]
    av = [_dot(jnp.concatenate([a_ak[i], a_rk[i]], axis=0), blk(v, c, h)) for i, (c, h) in enumerate(chains)]
    wu = [_dot(x[i], jnp.concatenate([blk(kk_t, c, h), av[i][0:C]], axis=1)) for i, (c, h) in enumerate(chains)]
    bwu = [_dot_tn(blk(b_e_m[h % 2], c, h), wu[i]) for i, (c, h) in enumerate(chains)]
    ktv = [_dot_tn(blk(k_e_m[h % 2], c, h), blk(v, c, h)) for c, h in chains]
    arb = [_dot(a_rb[i], wu[i]) for i in range(len(chains))]

    state = [state_ref[j] for j in range(npair)]
    out_rows = []
    for c in range(nch):
        out_lanes = []
        for j in range(npair):
            ge = c * RWKV_HEADS + 2 * j
            go = ge + 1
            bsum = bwu[ge] + bwu[go]
            tm = eye128 * e_tot[c * C:c * C + 1, j * LANES:(j + 1) * LANES] - bsum[:, 0:LANES] * bd128
            zc = (ktv[ge] + ktv[go] - bsum[:, LANES:2 * LANES]) * bd128
            q = blk(r_t, c, 2 * j) - jnp.where(first_half, arb[ge][:, 0:LANES], arb[go][:, 0:LANES])
            o_in = jnp.where(first_half, av[ge][C:2 * C] - arb[ge][:, LANES:2 * LANES],
                             av[go][C:2 * C] - arb[go][:, LANES:2 * LANES])
            ts = _dot(jnp.concatenate([tm, q], axis=0), state[j])
            state[j] = ts[0:LANES] + zc
            out_lanes.append(ts[LANES:LANES + C] + o_in)
        out_rows.append(jnp.concatenate(out_lanes, axis=1))
    for j in range(npair):
        state_ref[j] = state[j]
    o = jnp.concatenate(out_rows, axis=0)

    inv_n = 1.0 / N
    mean = head_sums(o) * inv_n
    oc = o - mean
    var = head_sums(oc * oc) * inv_n
    on = oc * lax.rsqrt(var + GN_EPS) * gnw_ref[...] + gnb_ref[...]
    bonus = head_sums(r * kmod * rk_ref[...]) * v
    o_ref[...] = (on + bonus) * g


def _rwkv(z_rw, mu_p, w0, w2, a0, a2, g2, k_k, k_a, r_k, gn_w, gn_b):
    B, T, _ = z_rw.shape
    L = RW_CHUNK * RW_NCH
    row = lambda t: t.reshape(1, RWKV_W)
    w2p = jnp.concatenate([w2, jnp.zeros((LANES - D_DECAY_LORA, RWKV_W), w2.dtype)], axis=0).astype(bf16)
    a2p = jnp.concatenate([a2, jnp.zeros((LANES - D_AAA_LORA, RWKV_W), a2.dtype)], axis=0).astype(bf16)
    const = lambda shp: pl.BlockSpec(shp, lambda b, c: (0,) * len(shp))
    return pl.pallas_call(
        _rwkv_body,
        grid=(B, T // L),
        in_specs=[pl.BlockSpec((None, L, RW_P), lambda b, c: (b, c, 0)),
                  const((1, RW_P)), const((1, RWKV_W)), const((LANES, RWKV_W)), const((1, RWKV_W)),
                  const((LANES, RWKV_W)), const((D_GATE_LORA, RWKV_W)), const((1, RWKV_W)), const((1, RWKV_W)),
                  const((1, RWKV_W)), const((1, RWKV_W)), const((1, RWKV_W))],
        out_specs=pl.BlockSpec((None, L, RWKV_W), lambda b, c: (b, c, 0)),
        out_shape=jax.ShapeDtypeStruct((B, T, RWKV_W), f32),
        scratch_shapes=[pltpu.VMEM((RWKV_W // LANES, LANES, LANES), f32),
                        pltpu.VMEM((SUBLANES, RW_P), f32)],
        compiler_params=_cparams("parallel", "arbitrary"),
        name="rwkv7",
    )(z_rw, mu_p, row(w0), w2p, row(a0), a2p, g2.astype(bf16), row(k_k), row(k_a), row(r_k), row(gn_w), row(gn_b))


def _cmp_body(x2_ref, w1k_ref, w1v_ref, c0k_ref, c0v_ref, w2k_ref, w2v_ref, ko_ref, vto_ref):
    half = w1k_ref.shape[0] // 2
    x2 = x2_ref[...]
    nrow = x2.shape[0]

    def mlp(w1_ref, c0_ref, w2_ref):
        ha = jnp.dot(x2, w1_ref[0:half, :], preferred_element_type=f32)
        hb = jnp.dot(x2, w1_ref[half:2 * half, :], preferred_element_type=f32)
        hb_next = pltpu.roll(hb, nrow - 1, 0)
        out = _dot(_gelu_tanh(ha + hb_next + c0_ref[...]), w2_ref[...])
        n = lax.broadcasted_iota(jnp.int32, out.shape, 0)
        return jnp.where(n < nrow - 1, out, 0.0)

    k_cmp = mlp(w1k_ref, c0k_ref, w2k_ref)
    n1 = lax.broadcasted_iota(jnp.int32, (nrow, 1), 0)
    tail = _key_tail(n1 * CMP_STRIDE + (CMP_LEN - 1), False)
    ko_ref[...] = (jnp.concatenate([k_cmp, jnp.zeros((nrow, LANES - HEAD_DIM), f32)], axis=1) + tail).astype(bf16)
    vto_ref[...] = mlp(w1v_ref, c0v_ref, w2v_ref).T.astype(bf16)


def _compress(x2, pe_k, pe_v, w1_k, w2_k, w1_v, w2_v):
    B, G, M, _ = x2.shape
    Dh = HEAD_DIM
    w3k = w1_k.reshape(CMP_LEN, Dh, CMP_HIDDEN)
    w3v = w1_v.reshape(CMP_LEN, Dh, CMP_HIDDEN)
    zeros = jnp.zeros_like(w3k)
    w1k_e = jnp.concatenate([w3k, zeros], axis=1).reshape(CMP_LEN * LANES, CMP_HIDDEN).astype(bf16)
    w1v_e = jnp.concatenate([zeros, w3v], axis=1).reshape(CMP_LEN * LANES, CMP_HIDDEN).astype(bf16)
    flat = CMP_LEN * Dh
    c0k = jnp.dot(pe_k.reshape(1, flat), w1_k, precision=HI)
    c0v = jnp.dot(pe_v.reshape(1, flat), w1_v, precision=HI)
    const = lambda shp: pl.BlockSpec(shp, lambda b, g: (0,) * len(shp))
    return pl.pallas_call(
        _cmp_body,
        grid=(B, G),
        in_specs=[pl.BlockSpec((None, None, M, CMP_STRIDE * LANES), lambda b, g: (b, g, 0, 0)),
                  const((CMP_LEN * LANES, CMP_HIDDEN)), const((CMP_LEN * LANES, CMP_HIDDEN)),
                  const((1, CMP_HIDDEN)), const((1, CMP_HIDDEN)),
                  const((CMP_HIDDEN, Dh)), const((CMP_HIDDEN, Dh))],
        out_specs=[pl.BlockSpec((None, None, M, LANES), lambda b, g: (b, g, 0, 0)),
                   pl.BlockSpec((None, None, Dh, M), lambda b, g: (b, g, 0, 0))],
        out_shape=[jax.ShapeDtypeStruct((B, G, M, LANES), bf16), jax.ShapeDtypeStruct((B, G, Dh, M), bf16)],
        compiler_params=_cparams("parallel", "parallel"),
        name="nsa_compress",
    )(x2, w1k_e, w1v_e, c0k, c0v, w2_k.astype(bf16), w2_v.astype(bf16))


def _nsa_body(q_ref, kc_ref, vct_ref, ks_ref, vst_ref, kw_ref, vwt_ref, gl_ref, ovt_ref,
              o_ref, score_ref, cnt_ref, q3_ref, selg_ref, s_sel, p_sel, acc_sel, s_win, p_win, acc_win):
    QT = NSA_QT
    KB = NSA_KB
    R = NSA_Q_PER_KV
    Dh = HEAD_DIM
    T = ks_ref.shape[0]
    NB = T // SEL_LEN
    NC = kc_ref.shape[0]
    qi = pl.program_id(2)
    t0 = qi * QT

    W = R * QT
    q3_ref[...] = jnp.concatenate([q_ref[r * LANES:(r + 1) * LANES, :] for r in range(R)], axis=1)
    slope_rows = q3_ref[Dh:Dh + NSA_AUG, :].astype(f32)[0:4]
    tq1 = t0 + lax.broadcasted_iota(jnp.int32, (1, QT), 1)
    tile3 = lambda t: jnp.concatenate([t] * R, axis=1)

    def col_reduce(x, op):
        parts = [x[i * SUBLANES:(i + 1) * SUBLANES] for i in range(x.shape[0] // SUBLANES)]
        while len(parts) > 1:
            nxt = [op(parts[2 * i], parts[2 * i + 1]) for i in range(len(parts) // 2)]
            parts = nxt + parts[2 * (len(parts) // 2):]
        return parts[0]

    colmax = lambda x: jnp.max(col_reduce(x, jnp.maximum), axis=0, keepdims=True)
    colsum = lambda x: jnp.sum(col_reduce(x, jnp.add), axis=0, keepdims=True)

    n_c = lax.broadcasted_iota(jnp.int32, (NC, 1), 0)
    vis_c = jnp.where(((n_c * CMP_STRIDE + (CMP_LEN - 1)) <= tq1) & (n_c < NC - 1), 0.0, NEG_INF)
    s_c = jnp.dot(kc_ref[...], q3_ref[...], preferred_element_type=f32) + tile3(vis_c)
    m_c = colmax(s_c)
    e_c = jnp.exp2(s_c - m_c)
    p_c = e_c * jnp.where(m_c > 0.5 * NEG_INF, 1.0 / colsum(e_c), 0.0)
    o_c = jnp.dot(vct_ref[...], p_c.astype(bf16), preferred_element_type=f32)
    p_sum = p_c[:, 0:QT]
    for r in range(1, R):
        p_sum = p_sum + p_c[:, r * QT:(r + 1) * QT]
    imp = _split_dot(p_sum, ovt_ref[...], 3, True)

    j = lax.broadcasted_iota(jnp.int32, (NB, 1), 0)
    cur = tq1 >> (SEL_LEN.bit_length() - 1)
    valid = (j * SEL_LEN) <= tq1
    forced = (j == 0) | (j == cur) | (j == cur - 1)
    score_ref[...] = jnp.where(valid, imp + FORCED_BONUS * forced.astype(f32), -jnp.inf)
    cnt_ref[...] = jnp.zeros_like(cnt_ref)
    nrb = NB // SUBLANES
    jsub = lax.broadcasted_iota(jnp.int32, (SUBLANES, 1), 0)
    last_rb = (t0 + QT - 1) // (SUBLANES * SEL_LEN)
    for ib in range(nrb):
        @pl.when(ib <= last_rb)
        def _(ib=ib):
            rows = [jnp.broadcast_to(score_ref[ib * SUBLANES + ii:ib * SUBLANES + ii + 1, :], (SUBLANES, QT))
                    for ii in range(SUBLANES)]
            for jb in range(nrb):
                @pl.when(jb <= last_rb)
                def _(jb=jb):
                    sj = score_ref[jb * SUBLANES:(jb + 1) * SUBLANES, :]
                    acc = cnt_ref[jb * SUBLANES:(jb + 1) * SUBLANES, :]
                    for ii in range(SUBLANES):
                        if ib < jb:
                            ahead = rows[ii] >= sj
                        elif ib > jb:
                            ahead = rows[ii] > sj
                        else:
                            ahead = (rows[ii] > sj) | ((rows[ii] == sj) & (ii < jsub))
                        acc = acc + jnp.where(ahead, 1.0, 0.0)
                    cnt_ref[jb * SUBLANES:(jb + 1) * SUBLANES, :] = acc
    selected = (cnt_ref[...] < float(min(SEL_TOPK, NB))) & valid
    sel_add = tile3(jnp.where(selected, 0.0, NEG_INF))
    spb = KB // SEL_LEN
    pad_rows = jnp.zeros((NSA_AUG - 4 - spb, W), f32)
    for tb in range(T // KB):
        selg_ref[tb] = jnp.concatenate([slope_rows, sel_add[tb * spb:(tb + 1) * spb], pad_rows], axis=0).astype(bf16)

    def stream(k_ref, vt_ref, n_plain, n_masked, start_of, mask_of, before_scores, s_buf, p_buf, acc_ref):
        n_all = n_plain + n_masked

        def scores(i, slot):
            ic = jnp.minimum(i, n_all - 1)
            before_scores(ic)
            s_buf[slot] = jnp.dot(k_ref[pl.ds(start_of(ic), KB), :], q3_ref[...], preferred_element_type=f32)

        def body(i, carry, masked, slot):
            m, l, alpha_prev = carry
            vblk = vt_ref[:, pl.ds(start_of(jnp.maximum(i - 1, 0)), KB)]
            pv = jnp.dot(vblk, p_buf[1 - slot], preferred_element_type=f32)
            scores(i + 1, 1 - slot)
            s = s_buf[slot]
            if masked:
                s = s + tile3(mask_of(i))
            m_new = jnp.maximum(m, colmax(s))
            alpha = jnp.exp2(m - m_new)
            p = jnp.exp2(s - m_new)
            p_buf[slot] = p.astype(bf16)
            acc_ref[...] = alpha_prev * acc_ref[...] + pv
            return m_new, alpha * l + colsum(p), alpha

        p_buf[1] = jnp.zeros((KB, W), bf16)
        acc_ref[...] = jnp.zeros_like(acc_ref)
        scores(0, 0)
        carry = (jnp.full((1, W), NEG_INF, f32), jnp.zeros((1, W), f32), jnp.ones((1, W), f32))

        def pair(j, carry, masked):
            return body(2 * j + 1, body(2 * j, carry, masked, 0), masked, 1)

        carry = lax.fori_loop(0, n_plain // 2, functools.partial(pair, masked=False), carry)
        _, l, alpha_last = lax.fori_loop(n_plain // 2, n_all // 2, functools.partial(pair, masked=True), carry)
        pv = jnp.dot(vt_ref[:, pl.ds(start_of(n_all - 1), KB)], p_buf[1], preferred_element_type=f32)
        return (alpha_last * acc_ref[...] + pv) * (1.0 / l)

    rel = lax.broadcasted_iota(jnp.int32, (1, QT), 1) - lax.broadcasted_iota(jnp.int32, (KB, 1), 0)

    nfull = qi * (QT // KB)

    def load_sel_rows(i):
        q3_ref[Dh:Dh + NSA_AUG, :] = selg_ref[i]

    o_s = stream(ks_ref, vst_ref, nfull, QT // KB,
                 lambda i: pl.multiple_of(i * KB, KB),
                 lambda i: jnp.where(rel - (i - nfull) * KB >= 0, 0.0, NEG_INF),
                 load_sel_rows, s_sel, p_sel, acc_sel)

    def win_mask(i):
        off = i * KB - WINDOW
        d = rel - off
        return jnp.where((d >= 0) & (d < WINDOW) & (t0 + off >= 0), 0.0, NEG_INF)

    o_w = stream(kw_ref, vwt_ref, 0, (WINDOW + QT) // KB,
                 lambda i: pl.multiple_of(jnp.maximum(t0 - WINDOW + i * KB, 0), KB),
                 win_mask, lambda i: None, s_win, p_win, acc_win)


    gates = _sigmoid(gl_ref[...])
    for r in range(R):
        ls = slice(r * QT, (r + 1) * QT)
        o_ref[r * Dh:(r + 1) * Dh, :] = (gates[3 * r:3 * r + 1, :] * o_c[:, ls]
                                          + gates[3 * r + 1:3 * r + 2, :] * o_s[:, ls]
                                          + gates[3 * r + 2:3 * r + 3, :] * o_w[:, ls]).astype(o_ref.dtype)


def _alibi_slopes(n):
    def pow2(m):
        start = 2.0 ** (-8.0 / m)
        return [start ** (i + 1) for i in range(m)]
    if math.log2(n).is_integer():
        return pow2(n)
    c = 2 ** math.floor(math.log2(n))
    return pow2(c) + pow2(2 * c)[0::2][: n - c]


def _nsa_attend(q_t, k_cmp, v_cmp_t, ks, vs_t, kw, vw_t, gl_t):
    B, G, T, _ = ks.shape
    Dh = HEAD_DIM
    QT = NSA_QT
    R = NSA_Q_PER_KV
    NB = T // SEL_LEN
    NC = k_cmp.shape[2]
    assert QT % NSA_KB == 0 and WINDOW % NSA_KB == 0 and T % QT == 0 and NSA_KB % SEL_LEN == 0
    n = np.arange(NC)
    jb = np.arange(NB)
    ov = ((n[:, None] * CMP_STRIDE < jb[None, :] * SEL_LEN + SEL_LEN)
          & (n[:, None] * CMP_STRIDE + CMP_LEN - 1 >= jb[None, :] * SEL_LEN) & (n[:, None] < NC - 1))
    ovt = jnp.asarray(ov.T.astype(np.float32))
    per_bg = lambda shp: pl.BlockSpec((None, None) + shp, lambda b, g, i: (b, g, 0, 0))
    const = lambda shp: pl.BlockSpec(shp, lambda b, g, i: (0, 0))
    return pl.pallas_call(
        _nsa_body,
        grid=(B, G, T // QT),
        in_specs=[pl.BlockSpec((None, R * LANES, QT), lambda b, g, i: (b, g, i)),
                  per_bg((NC, LANES)), per_bg((Dh, NC)),
                  per_bg((T, LANES)), per_bg((Dh, T)), per_bg((T, LANES)), per_bg((Dh, T)),
                  pl.BlockSpec((None, GL_ROWS, QT), lambda b, g, i: (b, g, i)),
                  const((NB, NC))],
        out_specs=pl.BlockSpec((None, R * Dh, QT), lambda b, g, i: (b, g, i)),
        out_shape=jax.ShapeDtypeStruct((B, G * R * Dh, T), bf16),
        scratch_shapes=[pltpu.VMEM((NB, QT), f32), pltpu.VMEM((NB, QT), f32), pltpu.VMEM((LANES, R * QT), bf16),
                        pltpu.VMEM((T // NSA_KB, NSA_AUG, R * QT), bf16),
                        pltpu.VMEM((2, NSA_KB, R * QT), f32), pltpu.VMEM((2, NSA_KB, R * QT), bf16),
                        pltpu.VMEM((Dh, R * QT), f32),
                        pltpu.VMEM((2, NSA_KB, R * QT), f32), pltpu.VMEM((2, NSA_KB, R * QT), bf16),
                        pltpu.VMEM((Dh, R * QT), f32)],
        compiler_params=_cparams("parallel", "parallel", "arbitrary"),
        name="nsa_attend",
    )(q_t, k_cmp, v_cmp_t, ks, vs_t, kw, vw_t, gl_t, ovt)


def _s5_body(u_ref, tg_ref, bend_ref, cpow_ref, apow_ref, y_ref):
    B, CH, T = u_ref.shape
    Cs = S5_CHUNK
    nk = T // Cs
    rows = nk * B
    conv = None
    inc = None
    for ci in range(CH):
        a = jnp.concatenate([u_ref[:, ci, k * Cs:(k + 1) * Cs] for k in range(nk)], axis=0).astype(bf16)
        c1 = jnp.dot(a, tg_ref[ci], preferred_element_type=f32)
        i1 = jnp.dot(a, bend_ref[ci], preferred_element_type=f32)
        conv = c1 if conv is None else conv + c1
        inc = i1 if inc is None else inc + i1
    kidx = lax.broadcasted_iota(jnp.int32, (rows, 1), 0) >> (B.bit_length() - 1)
    x = inc
    sh = 1
    lvl = 0
    while sh < nk:
        prev = jnp.where(kidx >= sh, pltpu.roll(x, sh * B, 0), 0.0)
        x = x + apow_ref[lvl, 0:1, :] * prev + apow_ref[lvl, 1:2, :] * pltpu.roll(prev, S5_STATE, 1)
        sh *= 2
        lvl += 1
    start = jnp.where(kidx >= 1, pltpu.roll(x, B, 0), 0.0)
    y = conv + jnp.dot(start.astype(bf16), cpow_ref[...], preferred_element_type=f32)
    for k in range(nk):
        for co in range(CH):
            y_ref[:, co, k * Cs:(k + 1) * Cs] = y[k * B:(k + 1) * B, co * Cs:(co + 1) * Cs]


def _s5_tables(lam_re, lam_im, log_dt, b_re, b_im, c_re, c_im):
    Cs = S5_CHUNK
    lam_re, lam_im = lam_re.astype(f32), lam_im.astype(f32)
    b_re, b_im, c_re, c_im = (t.astype(f32) for t in (b_re, b_im, c_re, c_im))
    dt = jnp.exp(log_dt.astype(f32))[:, None]
    mag = jnp.exp(lam_re * dt)
    ab_re, ab_im = mag * jnp.cos(lam_im * dt), mag * jnp.sin(lam_im * dt)
    den = lam_re * lam_re + lam_im * lam_im
    f_re = ((ab_re - 1.0) * lam_re + ab_im * lam_im) / den
    f_im = (ab_im * lam_re - (ab_re - 1.0) * lam_im) / den
    bb_re = f_re[..., None] * b_re - f_im[..., None] * b_im
    bb_im = f_re[..., None] * b_im + f_im[..., None] * b_re
    tau = jnp.arange(Cs + 1, dtype=f32)[None, None, :]
    pmag = jnp.exp(lam_re[..., None] * dt[..., None] * tau)
    pang = lam_im[..., None] * dt[..., None] * tau
    pw_re, pw_im = pmag * jnp.cos(pang), pmag * jnp.sin(pang)
    ein = functools.partial(jnp.einsum, precision=HI)
    ca_re = c_re[..., None] * pw_re[:, None] - c_im[..., None] * pw_im[:, None]
    ca_im = c_re[..., None] * pw_im[:, None] + c_im[..., None] * pw_re[:, None]
    kern = ein('gopt,gpi->gtoi', ca_re[..., :Cs], bb_re) - ein('gopt,gpi->gtoi', ca_im[..., :Cs], bb_im)
    s_idx = jnp.arange(Cs)
    lag_np = np.arange(Cs)[None, :] - np.arange(Cs)[:, None]
    place = jnp.asarray((lag_np[:, :, None] == np.arange(Cs)[None, None, :]).astype(np.float32)).astype(bf16)
    tg = jnp.einsum('stl,gloi->gisot', place, kern.astype(bf16), preferred_element_type=bf16)
    tg = tg.reshape(S5_GROUPS, S5_CH, Cs, S5_CH * Cs)
    rev = (Cs - 1) - s_idx
    pe_re, pe_im = pw_re[:, :, rev], pw_im[:, :, rev]
    be_re = pe_re[..., None] * bb_re[:, :, None, :] - pe_im[..., None] * bb_im[:, :, None, :]
    be_im = pe_re[..., None] * bb_im[:, :, None, :] + pe_im[..., None] * bb_re[:, :, None, :]
    bend = jnp.concatenate([jnp.transpose(be_re, (0, 3, 2, 1)), jnp.transpose(be_im, (0, 3, 2, 1))], axis=-1)
    cp_re = jnp.transpose(ca_re[..., 1:], (0, 2, 1, 3))
    cp_im = jnp.transpose(ca_im[..., 1:], (0, 2, 1, 3))
    cpow = jnp.concatenate([cp_re, -cp_im], axis=1).reshape(S5_GROUPS, 2 * S5_STATE, S5_CH * Cs)
    lv = (Cs * 2.0 ** jnp.arange(S5_SCAN_LEVELS, dtype=f32))[None, None, :]
    qmag = jnp.exp(lam_re[..., None] * dt[..., None] * lv)
    qang = lam_im[..., None] * dt[..., None] * lv
    q_re, q_im = jnp.transpose(qmag * jnp.cos(qang), (0, 2, 1)), jnp.transpose(qmag * jnp.sin(qang), (0, 2, 1))
    apow = jnp.stack([jnp.concatenate([q_re, q_re], axis=-1), jnp.concatenate([-q_im, q_im], axis=-1)], axis=2)
    return tg.astype(bf16), bend.astype(bf16), cpow.astype(bf16), apow


def _s5_scan(u_t, tables):
    B, _, T = u_t.shape
    Cs = S5_CHUNK
    nk = T // Cs
    assert B == SUBLANES and nk & (nk - 1) == 0 and nk <= 2 ** S5_SCAN_LEVELS
    tg, bend, cpow, apow = tables
    W = S5_CH * Cs
    per_g = lambda shp: pl.BlockSpec((None,) + shp, lambda g: (g,) + (0,) * len(shp))
    grp = pl.BlockSpec((B, S5_CH, T), lambda g: (0, g, 0))
    return pl.pallas_call(
        _s5_body,
        grid=(S5_GROUPS,),
        in_specs=[grp, per_g((S5_CH, Cs, W)), per_g((S5_CH, Cs, 2 * S5_STATE)), per_g((2 * S5_STATE, W)),
                  per_g((S5_SCAN_LEVELS, 2, 2 * S5_STATE))],
        out_specs=grp,
        out_shape=jax.ShapeDtypeStruct((B, S5_W, T), f32),
        compiler_params=_cparams("parallel"),
        name="s5_conv",
    )(u_t, tg, bend, cpow, apow)


def _s5_post_body(y_ref, u_ref, d_ref, w_ref, o_ref):
    y = _gelu_tanh(y_ref[...] + d_ref[...] * u_ref[...])
    vg = _dot_tn(y, w_ref[...])
    o_ref[...] = vg[:, 0:S5_W] * _sigmoid(vg[:, S5_W:2 * S5_W])


def _s5_post(y_t, u_t, d_skip, w_glu):
    B, _, T = y_t.shape
    tm = min(S5P_TM, T)
    blk = pl.BlockSpec((None, S5_W, tm), lambda b, i: (b, 0, i))
    return pl.pallas_call(
        _s5_post_body,
        grid=(B, T // tm),
        in_specs=[blk, blk, pl.BlockSpec((S5_W, 1), lambda b, i: (0, 0)),
                  pl.BlockSpec((S5_W, 2 * S5_W), lambda b, i: (0, 0))],
        out_specs=pl.BlockSpec((None, tm, S5_W), lambda b, i: (b, i, 0)),
        out_shape=jax.ShapeDtypeStruct((B, T, S5_W), f32),
        compiler_params=_cparams("parallel", "parallel"),
        name="s5_glu",
    )(y_t, u_t, d_skip.reshape(S5_W, 1), w_glu.astype(bf16))


def _out_proj_body(h_ref, orw_ref, onsat_ref, os5_ref, w_ref, g_ref, o_ref):
    mix = (_dot(orw_ref[...], w_ref[0:RWKV_W, :])
           + _dot_tn(onsat_ref[...], w_ref[RWKV_W:RWKV_W + NSA_W, :])
           + _dot(os5_ref[...], w_ref[RWKV_W + NSA_W:D_MIX, :]))
    o_ref[...] = h_ref[...] + _rms(mix, g_ref[...])


def _out_proj(h, o_rw, o_nsa_t, o_s5, w_out, g):
    B, T, _ = h.shape
    tm = min(OUT_TM, T)
    rows = lambda w: pl.BlockSpec((None, tm, w), lambda b, i: (b, i, 0))
    return pl.pallas_call(
        _out_proj_body,
        grid=(B, T // tm),
        in_specs=[rows(D_MODEL), rows(RWKV_W), pl.BlockSpec((None, NSA_W, tm), lambda b, i: (b, 0, i)), rows(S5_W),
                  pl.BlockSpec((D_MIX, D_MODEL), lambda b, i: (0, 0)),
                  pl.BlockSpec((1, D_MODEL), lambda b, i: (0, 0))],
        out_specs=rows(D_MODEL),
        out_shape=jax.ShapeDtypeStruct((B, T, D_MODEL), f32),
        compiler_params=_cparams("parallel", "parallel"),
        name="out_proj",
    )(h, o_rw, o_nsa_t, o_s5, w_out.astype(bf16), g.reshape(1, D_MODEL))


def _ffn_body(h_ref, hp_ref, gpre_ref, wg_ref, wu_ref, cwg_ref, cwu_ref, cbg_ref, cbu_ref, wd_ref, gpost_ref,
              o_ref, xn_ref, acc_ref, *, tiles_per_seq):
    tm = h_ref.shape[0]
    H = SUBLANES
    c = pl.program_id(1)

    @pl.when(c == 0)
    def _():
        first = (pl.program_id(0) % tiles_per_seq) == 0
        xp = _rms(hp_ref[...], gpre_ref[...])
        xn_ref[0:H, :] = jnp.where(first, 0.0, xp).astype(bf16)
        xn_ref[H:H + tm, :] = _rms(h_ref[...], gpre_ref[...]).astype(bf16)
        acc_ref[...] = jnp.zeros_like(acc_ref)

    xn = xn_ref[...]

    def conv_branch(w_ref, cw_ref, cb_ref):
        hu = jnp.dot(xn, w_ref[...], preferred_element_type=f32)
        cw = cw_ref[...]
        out = (cw[0:1, :] * pltpu.roll(hu, 2, 0)[H:H + tm]
               + cw[1:2, :] * pltpu.roll(hu, 1, 0)[H:H + tm]
               + cw[2:3, :] * hu[H:H + tm])
        return out + cb_ref[...]

    gate = conv_branch(wg_ref, cwg_ref, cbg_ref)
    up = conv_branch(wu_ref, cwu_ref, cbu_ref)
    acc_ref[...] += _dot(_gelu_tanh(gate) * up, wd_ref[...])

    @pl.when(c == pl.num_programs(1) - 1)
    def _():
        o_ref[...] = h_ref[...] + _rms(acc_ref[...], gpost_ref[...])


def _conv_ffn(h2d, seq_len, g_pre, w_up, conv_w, conv_b, w_down, g_post):
    n = h2d.shape[0]
    tm = min(FFN_TM, seq_len)
    tc = FFN_TC
    nc = D_FF // tc
    H = SUBLANES
    tiles_per_seq = seq_len // tm
    hpb = tm // H
    w_up = w_up.astype(bf16)
    cb = conv_b.reshape(1, 2 * D_FF)
    return pl.pallas_call(
        functools.partial(_ffn_body, tiles_per_seq=tiles_per_seq),
        grid=(n // tm, nc),
        in_specs=[pl.BlockSpec((tm, D_MODEL), lambda i, c: (i, 0)),
                  pl.BlockSpec((H, D_MODEL), lambda i, c: (jnp.maximum(i * hpb - 1, 0), 0)),
                  pl.BlockSpec((1, D_MODEL), lambda i, c: (0, 0)),
                  pl.BlockSpec((D_MODEL, tc), lambda i, c: (0, c)),
                  pl.BlockSpec((D_MODEL, tc), lambda i, c: (0, nc + c)),
                  pl.BlockSpec((CONV_W, tc), lambda i, c: (0, c)),
                  pl.BlockSpec((CONV_W, tc), lambda i, c: (0, nc + c)),
                  pl.BlockSpec((1, tc), lambda i, c: (0, c)),
                  pl.BlockSpec((1, tc), lambda i, c: (0, nc + c)),
                  pl.BlockSpec((tc, D_MODEL), lambda i, c: (c, 0)),
                  pl.BlockSpec((1, D_MODEL), lambda i, c: (0, 0))],
        out_specs=pl.BlockSpec((tm, D_MODEL), lambda i, c: (i, 0)),
        out_shape=jax.ShapeDtypeStruct((n, D_MODEL), f32),
        scratch_shapes=[pltpu.VMEM((H + tm, D_MODEL), bf16), pltpu.VMEM((tm, D_MODEL), f32)],
        compiler_params=_cparams("parallel", "arbitrary"),
        name="conv_ffn",
    )(h2d, h2d, g_pre.reshape(1, D_MODEL), w_up, w_up, conv_w, conv_w, cb, cb, w_down.astype(bf16),
      g_post.reshape(1, D_MODEL))


def _ple_body(h_ref, p_ref, wp_ref, gp_ref, wg_ref, o_ref):
    h = h_ref[...]
    e = _rms(_dot(p_ref[...], wp_ref[...]), gp_ref[...])
    gate = _sigmoid(_dot(h, wg_ref[...]))
    o_ref[...] = h + gate * e


def _ple(h2d, p2d, w_ple, g_ple, w_gate):
    n = h2d.shape[0]
    tm = min(PLE_TM, n)
    return pl.pallas_call(
        _ple_body,
        grid=(n // tm,),
        in_specs=[pl.BlockSpec((tm, D_MODEL), lambda i: (i, 0)),
                  pl.BlockSpec((tm, PLE_DIM), lambda i: (i, 0)),
                  pl.BlockSpec((PLE_DIM, D_MODEL), lambda i: (0, 0)),
                  pl.BlockSpec((1, D_MODEL), lambda i: (0, 0)),
                  pl.BlockSpec((D_MODEL, D_MODEL), lambda i: (0, 0))],
        out_specs=pl.BlockSpec((tm, D_MODEL), lambda i: (i, 0)),
        out_shape=jax.ShapeDtypeStruct((n, D_MODEL), f32),
        compiler_params=_cparams("parallel"),
        name="ple_gate",
    )(h2d, p2d, w_ple.astype(bf16), g_ple.reshape(1, D_MODEL), w_gate.astype(bf16))


def kernel(x, p, pre_mix_norm, post_mix_norm, pre_ffn_norm, post_ffn_norm, w_in, w_out, shift_mu, rw_w0, rw_w2, rw_a0, rw_a2, rw_g2, rw_k_k, rw_k_a, rw_r_k, rw_gn_w, rw_gn_b, cmp_pe_k, cmp_pe_v, cmp_w1_k, cmp_w2_k, cmp_w1_v, cmp_w2_v, s5_lam_re, s5_lam_im, s5_log_dt, s5_b_re, s5_b_im, s5_c_re, s5_c_im, s5_d, s5_w_glu, w_up, conv_w, conv_b, w_down, w_ple, ple_norm, w_ple_gate):
    B, T, D = x.shape
    n = B * T
    depth = w_in.shape[0]
    h = x
    for i in range(depth):
        wn, wt, qaug, mu_p = _prep_in_weights(w_in[i], shift_mu[i])
        z_rw, z_s5, ks, kw, kcv, q_t, vs_t, vw_t, gl_t = _in_proj(h, pre_mix_norm[i], wn, wt, qaug)
        o_rw = _rwkv(z_rw, mu_p, rw_w0[i], rw_w2[i], rw_a0[i], rw_a2[i], rw_g2[i],
                     rw_k_k[i], rw_k_a[i], rw_r_k[i], rw_gn_w[i], rw_gn_b[i])
        k_cmp, v_cmp_t = _compress(kcv, cmp_pe_k[i], cmp_pe_v[i], cmp_w1_k[i], cmp_w2_k[i], cmp_w1_v[i], cmp_w2_v[i])
        o_nsa_t = _nsa_attend(q_t, k_cmp, v_cmp_t, ks, vs_t, kw, vw_t, gl_t)
        tables = _s5_tables(s5_lam_re[i], s5_lam_im[i], s5_log_dt[i], s5_b_re[i], s5_b_im[i], s5_c_re[i], s5_c_im[i])
        o_s5 = _s5_post(_s5_scan(z_s5, tables), z_s5, s5_d[i], s5_w_glu[i])
        h = _out_proj(h, o_rw, o_nsa_t, o_s5, w_out[i], post_mix_norm[i])
        h = _conv_ffn(h.reshape(n, D), T, pre_ffn_norm[i], w_up[i], conv_w[i], conv_b[i], w_down[i], post_ffn_norm[i])
        h = _ple(h, p[i].reshape(n, PLE_DIM), w_ple[i], ple_norm[i], w_ple_gate[i]).reshape(B, T, D)
    return h
```
